```python
import math
import jax, jax.numpy as jnp
from jax import lax
import numpy as np

D_MODEL = 1024
BATCH = 8
SEQ = 2048
DEPTH = 4
DEC_BATCH = 128
DEC_SEQ = 8
PAST_LEN = 8192
PAGE_SIZE = 128

N_MIXERS = 2
N_DIFF = (DEPTH + 1) // 2
N_MLA = DEPTH // 2
DA_HEADS = 8
DA_KV_HEADS = 4
DA_GROUP = DA_HEADS // DA_KV_HEADS
DA_HEAD_DIM = D_MODEL // (2 * DA_HEADS)
DA_ROT = DA_HEAD_DIM // 4
ROPE_THETA = 500000.0
MLA_HEADS = 8
MLA_Q_RANK = 256
MLA_KV_RANK = 128
MLA_NOPE = 128
MLA_ROPE = 64
MLA_V = 128
MLA_THETA = 10000.0
D_FF = 2816
CONV_W = 3
Q_BLOCK = 128
EPS = 1e-6

kernel_name = 'hybrid_diffattn_mla_convffn_step'


def _rms(x, g):
    xf = x.astype(jnp.float32)
    y = xf * lax.rsqrt(jnp.mean(xf * xf, axis=-1, keepdims=True) + EPS)
    return (y * g.astype(jnp.float32)).astype(x.dtype)


def _rope(x, pos, rot_dim, theta):
    half = rot_dim // 2
    inv = 1.0 / (theta ** (jnp.arange(half, dtype=jnp.float32) / half))
    ang = pos.astype(jnp.float32)[:, None] * inv[None, :]
    shape = (1, x.shape[1]) + (1,) * (x.ndim - 3) + (half,)
    cos = jnp.cos(ang).reshape(shape)
    sin = jnp.sin(ang).reshape(shape)
    xf = x.astype(jnp.float32)
    x1, x2, rest = xf[..., :half], xf[..., half:rot_dim], xf[..., rot_dim:]
    out = jnp.concatenate([x1 * cos - x2 * sin, x2 * cos + x1 * sin, rest], axis=-1)
    return out.astype(x.dtype)


def _prompt_attend(score_fn, pv_fn, seq):
    kpos = jnp.arange(seq)
    def block(i):
        qpos = i * Q_BLOCK + jnp.arange(Q_BLOCK)
        s = score_fn(i * Q_BLOCK)
        s = jnp.where(kpos[None, :] <= qpos[:, None], s, -jnp.inf)
        return pv_fn(jax.nn.softmax(s, axis=-1))
    return lax.map(block, jnp.arange(seq // Q_BLOCK))


def _online_attend(s0, v0, pv, page_scores, page_table):
    m = jnp.max(s0, axis=-1)
    p = jnp.exp(s0 - m[..., None])
    carry0 = (m, jnp.sum(p, axis=-1), pv(p, v0))
    def step(carry, phys):
        m, l, acc = carry
        s, v = page_scores(phys)
        m_new = jnp.maximum(m, jnp.max(s, axis=-1))
        corr = jnp.exp(m - m_new)
        p = jnp.exp(s - m_new[..., None])
        return (m_new, l * corr + jnp.sum(p, axis=-1), acc * corr[..., None] + pv(p, v)), None
    (m, l, acc), _ = lax.scan(step, carry0, page_table.T)
    return acc / l[..., None]


def _diff_qkv(h, pos, w_in, qk_g):
    B, S, _ = h.shape
    nq = DA_HEADS * 2 * DA_HEAD_DIM
    nk = DA_KV_HEADS * 2 * DA_HEAD_DIM
    q, k, v = jnp.split(h @ w_in, [nq, nq + nk], axis=-1)
    q = q.reshape(B, S, DA_KV_HEADS, DA_GROUP, 2, DA_HEAD_DIM)
    k = k.reshape(B, S, DA_KV_HEADS, 2, DA_HEAD_DIM)
    v = v.reshape(B, S, DA_KV_HEADS, 2 * DA_HEAD_DIM)
    q = _rope(_rms(q, qk_g[0]), pos, DA_ROT, ROPE_THETA)
    k = _rope(_rms(k, qk_g[1]), pos, DA_ROT, ROPE_THETA)
    return q, k, v


def _diff_out(o, sub_g, lam_init, w_out, dtype):
    o = _rms(o.astype(dtype), sub_g) * (1.0 - lam_init)
    return o.reshape(o.shape[0], o.shape[1], -1) @ w_out


def _diff_layer(hp, hs, pos_p, pos_s, cache_k, cache_v, li, page_table, w_in, qk_g, lam_p, sub_g, w_out, lam_init):
    scale = DA_HEAD_DIM ** -0.5
    lp = lam_p.astype(jnp.float32)
    lam = jnp.exp(jnp.sum(lp[0] * lp[1])) - jnp.exp(jnp.sum(lp[2] * lp[3])) + lam_init
    score = lambda q, k: jnp.einsum('bqhgmd,bkhmd->bhgmqk', q, k, preferred_element_type=jnp.float32) * scale
    pv = lambda p, v: jnp.einsum('bhgmqk,bkhe->bhgmqe', p, v)
    merge = lambda o: o[:, :, :, 0] - lam * o[:, :, :, 1]
    B, S = hp.shape[:2]
    qp, kp, vp = _diff_qkv(hp, pos_p, w_in, qk_g)
    op = _prompt_attend(lambda st: score(lax.dynamic_slice_in_dim(qp, st, Q_BLOCK, axis=1), kp),
                        lambda p: merge(pv(p, vp)), S)
    op = op.transpose(1, 0, 4, 2, 3, 5).reshape(B, S, DA_HEADS, 2 * DA_HEAD_DIM)
    Bd, T = hs.shape[:2]
    qs, ks, vs = _diff_qkv(hs, pos_s, w_in, qk_g)
    causal = jnp.tril(jnp.ones((T, T), dtype=bool))
    s0 = jnp.where(causal, score(qs, ks), -jnp.inf)
    os_ = _online_attend(s0, vs, pv, lambda phys: (score(qs, cache_k[li, phys]), cache_v[li, phys]), page_table)
    os_ = merge(os_).transpose(0, 3, 1, 2, 4).reshape(Bd, T, DA_HEADS, 2 * DA_HEAD_DIM)
    return (_diff_out(op, sub_g, lam_init, w_out, hp.dtype), _diff_out(os_, sub_g, lam_init, w_out, hs.dtype), kp, vp, ks, vs)


def _mla_proj(h, pos, w_in, q_a_g, kv_a_g, w_uq, qk_g):
    B, S, _ = h.shape
    cq, ckv, kr = jnp.split(h @ w_in, [MLA_Q_RANK, MLA_Q_RANK + MLA_KV_RANK], axis=-1)
    q = (_rms(cq, q_a_g) @ w_uq).reshape(B, S, MLA_HEADS, MLA_NOPE + MLA_ROPE)
    qn = _rms(q[..., :MLA_NOPE], qk_g[0, :MLA_NOPE])
    qr = _rope(_rms(q[..., MLA_NOPE:], qk_g[0, MLA_NOPE:]), pos, MLA_ROPE, MLA_THETA)
    kr = _rope(_rms(kr, qk_g[1, MLA_NOPE:]), pos, MLA_ROPE, MLA_THETA)
    return qn, qr, _rms(ckv, kv_a_g), kr


def _mla_expand(ckv, w_ukv, k_g):
    kv = jnp.einsum('btr,rhe->bthe', ckv, w_ukv)
    return _rms(kv[..., :MLA_NOPE], k_g), kv[..., MLA_NOPE:]


def _mla_layer(hp, hs, pos_p, pos_s, cache_ckv, cache_kpe, li, page_table, w_in, q_a_g, kv_a_g, w_uq, w_ukv, qk_g, w_out):
    scale = (MLA_NOPE + MLA_ROPE) ** -0.5
    k_g = qk_g[1, :MLA_NOPE]
    def score(qn, qr, kn, kr):
        return (jnp.einsum('bqhd,bkhd->bhqk', qn, kn, preferred_element_type=jnp.float32)
                + jnp.einsum('bqhd,bkd->bhqk', qr, kr, preferred_element_type=jnp.float32)) * scale
    pv = lambda p, v: jnp.einsum('bhqk,bkhe->bhqe', p, v)
    B, S = hp.shape[:2]
    qnp, qrp, ckvp, krp = _mla_proj(hp, pos_p, w_in, q_a_g, kv_a_g, w_uq, qk_g)
    knp, vp = _mla_expand(ckvp, w_ukv, k_g)
    dyn = lambda a, st: lax.dynamic_slice_in_dim(a, st, Q_BLOCK, axis=1)
    op = _prompt_attend(lambda st: score(dyn(qnp, st), dyn(qrp, st), knp, krp), lambda p: pv(p, vp), S)
    op = op.transpose(1, 0, 3, 2, 4).reshape(B, S, MLA_HEADS * MLA_V)
    Bd, T = hs.shape[:2]
    qns, qrs, ckvs, krs = _mla_proj(hs, pos_s, w_in, q_a_g, kv_a_g, w_uq, qk_g)
    kns, vs = _mla_expand(ckvs, w_ukv, k_g)
    causal = jnp.tril(jnp.ones((T, T), dtype=bool))
    s0 = jnp.where(causal, score(qns, qrs, kns, krs), -jnp.inf)
    def page_scores(phys):
        kn, v = _mla_expand(cache_ckv[li, phys], w_ukv, k_g)
        return score(qns, qrs, kn, cache_kpe[li, phys]), v
    os_ = _online_attend(s0, vs, pv, page_scores, page_table)
    os_ = os_.transpose(0, 2, 1, 3).reshape(Bd, T, MLA_HEADS * MLA_V)
    return (op.astype(hp.dtype) @ w_out, os_.astype(hs.dtype) @ w_out, ckvp, krp, ckvs, krs)


def _conv_ffn(h, prev, w_gu, conv_w, conv_b, w_down):
    S = h.shape[1]
    g, u = jnp.split(h @ w_gu, 2, axis=-1)
    gp = jnp.concatenate([prev.astype(g.dtype), g], axis=1)
    gc = conv_b
    for j in range(CONV_W):
        gc = gc + conv_w[j] * gp[:, j:j + S]
    y = (jax.nn.silu(gc) * u) @ w_down
    return y, gp[:, -(CONV_W - 1):]


def setup_inputs(seed: int = 0) -> dict:
    key = jax.random.key(seed)
    ks = jax.random.split(key, 32)
    f32 = jnp.float32
    n_pages = PAST_LEN // PAGE_SIZE
    n_pool = (DEC_BATCH * n_pages * 5) // 4
    nrm = lambda k, shape, s: jax.random.normal(k, shape, f32) * s
    gain = lambda k, shape: 1.0 + 0.02 * jax.random.normal(k, shape, f32)
    da_in = DA_HEADS * 2 * DA_HEAD_DIM + 2 * DA_KV_HEADS * 2 * DA_HEAD_DIM
    mla_in = MLA_Q_RANK + MLA_KV_RANK + MLA_ROPE
    page_table = jax.random.permutation(ks[7], n_pool)[: DEC_BATCH * n_pages].reshape(DEC_BATCH, n_pages).astype(jnp.int32)
    return {
        'x_prompt': nrm(ks[0], (BATCH, SEQ, D_MODEL), 1.0),
        'x_sample': nrm(ks[1], (DEC_BATCH, DEC_SEQ, D_MODEL), 1.0),
        'cache_da_k': nrm(ks[2], (N_DIFF, n_pool, PAGE_SIZE, DA_KV_HEADS, 2, DA_HEAD_DIM), 1.0),
        'cache_da_v': nrm(ks[3], (N_DIFF, n_pool, PAGE_SIZE, DA_KV_HEADS, 2 * DA_HEAD_DIM), 1.0),
        'cache_mla_ckv': nrm(ks[4], (N_MLA, n_pool, PAGE_SIZE, MLA_KV_RANK), 1.0),
        'cache_mla_kpe': nrm(ks[5], (N_MLA, n_pool, PAGE_SIZE, MLA_ROPE), 1.0),
        'state_conv': nrm(ks[6], (DEPTH, DEC_BATCH, CONV_W - 1, D_FF), 1.0),
        'page_table': page_table,
        'attn_norm': gain(ks[8], (DEPTH, D_MODEL)),
        'ffn_norm': gain(ks[9], (DEPTH, D_MODEL)),
        'da_w_in': nrm(ks[10], (N_DIFF, D_MODEL, da_in), D_MODEL ** -0.5),
        'da_qk_norm': gain(ks[11], (N_DIFF, 2, DA_HEAD_DIM)),
        'da_lambda': nrm(ks[12], (N_DIFF, 4, DA_HEAD_DIM), 0.1),
        'da_sub_norm': gain(ks[13], (N_DIFF, 2 * DA_HEAD_DIM)),
        'da_w_out': nrm(ks[14], (N_DIFF, DA_HEADS * 2 * DA_HEAD_DIM, D_MODEL), (DA_HEADS * 2 * DA_HEAD_DIM) ** -0.5),
        'mla_w_in': nrm(ks[15], (N_MLA, D_MODEL, mla_in), D_MODEL ** -0.5),
        'mla_q_a_norm': gain(ks[16], (N_MLA, MLA_Q_RANK)),
        'mla_kv_a_norm': gain(ks[17], (N_MLA, MLA_KV_RANK)),
        'mla_w_uq': nrm(ks[18], (N_MLA, MLA_Q_RANK, MLA_HEADS * (MLA_NOPE + MLA_ROPE)), MLA_Q_RANK ** -0.5),
        'mla_w_ukv': nrm(ks[19], (N_MLA, MLA_KV_RANK, MLA_HEADS, MLA_NOPE + MLA_V), MLA_KV_RANK ** -0.5),
        'mla_qk_norm': gain(ks[20], (N_MLA, 2, MLA_NOPE + MLA_ROPE)),
        'mla_w_out': nrm(ks[21], (N_MLA, MLA_HEADS * MLA_V, D_MODEL), (MLA_HEADS * MLA_V) ** -0.5),
        'ffn_w_gu': nrm(ks[22], (DEPTH, D_MODEL, 2 * D_FF), D_MODEL ** -0.5),
        'ffn_conv_w': nrm(ks[23], (DEPTH, CONV_W, D_FF), CONV_W ** -0.5),
        'ffn_conv_b': nrm(ks[24], (DEPTH, D_FF), 0.02),
        'ffn_w_down': nrm(ks[25], (DEPTH, D_FF, D_MODEL), D_FF ** -0.5),
    }


def reference(x_prompt, x_sample, cache_da_k, cache_da_v, cache_mla_ckv, cache_mla_kpe, state_conv, page_table,
              attn_norm, ffn_norm, da_w_in, da_qk_norm, da_lambda, da_sub_norm, da_w_out,
              mla_w_in, mla_q_a_norm, mla_kv_a_norm, mla_w_uq, mla_w_ukv, mla_qk_norm, mla_w_out,
              ffn_w_gu, ffn_conv_w, ffn_conv_b, ffn_w_down):
    past = page_table.shape[1] * PAGE_SIZE
    pos_p = jnp.arange(x_prompt.shape[1])
    pos_s = past + jnp.arange(x_sample.shape[1])
    xp, xs = x_prompt, x_sample
    dkp, dvp, ckp, kep, cvp = [], [], [], [], []
    dks, dvs, cks, kes, cvs = [], [], [], [], []
    for i in range(DEPTH):
        hp = _rms(xp, attn_norm[i])
        hs = _rms(xs, attn_norm[i])
        li = i // N_MIXERS
        if i % N_MIXERS == 0:
            lam_init = 0.8 - 0.6 * math.exp(-0.3 * i)
            ap, a_s, k_p, v_p, k_s, v_s = _diff_layer(hp, hs, pos_p, pos_s, cache_da_k, cache_da_v, li, page_table,
                                                      da_w_in[li], da_qk_norm[li], da_lambda[li], da_sub_norm[li],
                                                      da_w_out[li], lam_init)
            dkp.append(k_p); dvp.append(v_p); dks.append(k_s); dvs.append(v_s)
        else:
            ap, a_s, c_p, r_p, c_s, r_s = _mla_layer(hp, hs, pos_p, pos_s, cache_mla_ckv, cache_mla_kpe, li, page_table,
                                                     mla_w_in[li], mla_q_a_norm[li], mla_kv_a_norm[li], mla_w_uq[li],
                                                     mla_w_ukv[li], mla_qk_norm[li], mla_w_out[li])
            ckp.append(c_p); kep.append(r_p); cks.append(c_s); kes.append(r_s)
        xp = xp + ap
        xs = xs + a_s
        zero_prev = jnp.zeros((xp.shape[0], CONV_W - 1, D_FF), xp.dtype)
        fp, sp = _conv_ffn(_rms(xp, ffn_norm[i]), zero_prev, ffn_w_gu[i], ffn_conv_w[i], ffn_conv_b[i], ffn_w_down[i])
        fs, ss = _conv_ffn(_rms(xs, ffn_norm[i]), state_conv[i], ffn_w_gu[i], ffn_conv_w[i], ffn_conv_b[i], ffn_w_down[i])
        cvp.append(sp); cvs.append(ss)
        xp = xp + fp
        xs = xs + fs
    return (xp, xs, jnp.stack(dkp), jnp.stack(dvp), jnp.stack(ckp), jnp.stack(kep), jnp.stack(cvp),
            jnp.stack(dks), jnp.stack(dvs), jnp.stack(cks), jnp.stack(kes), jnp.stack(cvs))
```

```python
import functools
import math

import jax
import jax.numpy as jnp
from jax import lax
from jax.experimental import pallas as pl
from jax.experimental.pallas import tpu as pltpu

F32 = jnp.float32
BF16 = jnp.bfloat16

EPS = 1e-6
LANES = 128
SUBLANES = 8
PAGE = 128
MASKED = -1e30
VMEM_LIMIT = 56 * 1024 * 1024

D_MODEL = 1024
DA_KV_HEADS = 4
DA_HEAD_DIM = 64
DA_ROT_HALF = 8
DA_THETA = 500000.0
MLA_HEADS = 8
MLA_Q_RANK = 256
MLA_KV_RANK = 128
MLA_NOPE = 128
MLA_ROPE = 64
MLA_V = 128
MLA_THETA = 10000.0
D_FF = 2816
N_MIXERS = 2

TM_PROJ = 512
TQ = 512
TM_FFN_PROMPT = 1024
TM_FFN_SAMPLE = 512
TF = 256
DA_PAGES_PER_STEP = 8
MLA_PAGES_PER_STEP = 16
MLA_CHUNK = 512

_NT = (((1,), (1,)), ((), ()))


def _params(*semantics):
    return pltpu.CompilerParams(dimension_semantics=semantics, vmem_limit_bytes=VMEM_LIMIT)


def _rms_rows(x, g):
    return x * lax.rsqrt(jnp.mean(x * x, axis=-1, keepdims=True) + EPS) * g


def _group_sumsq(x, group):
    x2 = x * x
    if group == LANES:
        return jnp.sum(x2, axis=-1, keepdims=True)
    shift = int(math.log2(group))
    r = lax.shift_right_logical(lax.broadcasted_iota(jnp.int32, (LANES, LANES), 0), shift)
    c = lax.shift_right_logical(lax.broadcasted_iota(jnp.int32, (LANES, LANES), 1), shift)
    ones = jnp.where(r == c, 1.0, 0.0).astype(BF16)
    hi = x2.astype(BF16)
    lo = (x2 - hi.astype(F32)).astype(BF16)
    return (jnp.dot(hi, ones, preferred_element_type=F32)
            + jnp.dot(lo, ones, preferred_element_type=F32))


def _group_rms(x, g, group):
    return x * lax.rsqrt(_group_sumsq(x, group) * (1.0 / group) + EPS) * g


def _rope(x, cos, sin_up, sin_dn, half):
    x_up = pltpu.roll(x, LANES - half, axis=1)
    x_dn = pltpu.roll(x, half, axis=1)
    return x * cos + x_up * sin_up + x_dn * sin_dn


def _online_update(s, m_prev, l_prev):
    m_new = jnp.maximum(m_prev, jnp.max(s, axis=-1, keepdims=True))
    corr = jnp.exp(m_prev - m_new)
    p = jnp.exp(s - m_new)
    l_new = l_prev * corr + jnp.sum(p, axis=-1, keepdims=True)
    return p, corr, m_new, l_new


def _proj_da_kernel(x_ref, gn_ref, w_ref, gq_ref, gk_ref, cos_ref, sup_ref, sdn_ref,
                    q_ref, k_ref, v_ref, kb_ref, vb_ref):
    tm = x_ref.shape[0]
    h = _rms_rows(x_ref[...], gn_ref[...]).astype(BF16)
    y = jnp.dot(h, w_ref[...], preferred_element_type=F32)
    cos, sup, sdn = cos_ref[...], sup_ref[...], sdn_ref[...]
    low = lax.broadcasted_iota(jnp.int32, (tm, LANES), 1) < DA_HEAD_DIM
    scale = DA_HEAD_DIM ** -0.5

    def norm_rope(blk, g):
        return _rope(_group_rms(blk, g, DA_HEAD_DIM), cos, sup, sdn, DA_ROT_HALF)

    gq, gk = gq_ref[...], gk_ref[...]
    for c in range(2 * DA_KV_HEADS):
        qb = norm_rope(y[:, c * LANES:(c + 1) * LANES], gq) * scale
        kvh, grp = divmod(c, 2)
        for mp in range(2):
            oc = (kvh * 2 + mp) * 2 + grp
            qm = jnp.where(low if mp == 0 else jnp.logical_not(low), qb, 0.0)
            q_ref[:, oc * LANES:(oc + 1) * LANES] = qm.astype(q_ref.dtype)
    k_off = 2 * DA_KV_HEADS * LANES
    for c in range(DA_KV_HEADS):
        kb = norm_rope(y[:, k_off + c * LANES:k_off + (c + 1) * LANES], gk)
        k_ref[:, c * LANES:(c + 1) * LANES] = kb
        kb_ref[:, c * LANES:(c + 1) * LANES] = kb.astype(BF16)
    v = y[:, k_off + DA_KV_HEADS * LANES:]
    v_ref[...] = v
    vb_ref[...] = v.astype(BF16)


def _proj_da(x, gn, w_in, gq, gk, tables, q_dtype):
    n = x.shape[0]
    tm = TM_PROJ
    cos, sup, sdn = tables
    t_blocks = cos.shape[0] // tm
    row = lambda i: (i, 0)
    const = lambda i: (0, 0)
    tab = lambda i: (i % t_blocks, 0)
    nq, nk = 4 * DA_KV_HEADS * 2 * LANES, DA_KV_HEADS * LANES
    return pl.pallas_call(
        _proj_da_kernel,
        grid=(n // tm,),
        in_specs=[pl.BlockSpec((tm, D_MODEL), row), pl.BlockSpec((1, D_MODEL), const),
                  pl.BlockSpec(w_in.shape, const), pl.BlockSpec((1, LANES), const),
                  pl.BlockSpec((1, LANES), const), pl.BlockSpec((tm, LANES), tab),
                  pl.BlockSpec((tm, LANES), tab), pl.BlockSpec((tm, LANES), tab)],
        out_specs=[pl.BlockSpec((tm, nq), row), pl.BlockSpec((tm, nk), row), pl.BlockSpec((tm, nk), row),
                   pl.BlockSpec((tm, nk), row), pl.BlockSpec((tm, nk), row)],
        out_shape=[jax.ShapeDtypeStruct((n, nq), q_dtype), jax.ShapeDtypeStruct((n, nk), F32),
                   jax.ShapeDtypeStruct((n, nk), F32), jax.ShapeDtypeStruct((n, nk), BF16),
                   jax.ShapeDtypeStruct((n, nk), BF16)],
        compiler_params=_params("parallel"),
        name="proj_da",
    )(x, gn, w_in, gq, gk, cos, sup, sdn)


def _proj_mla_kernel(x_ref, gn_ref, win_ref, qag_ref, kvag_ref, wuq_ref, gqn_ref, gqr_ref, gkr_ref,
                     kg_ref, cos_ref, sup_ref, sdn_ref, wk_ref, q_ref, ckv_ref, kr_ref, *rest, absorb):
    tm = x_ref.shape[0]
    h = _rms_rows(x_ref[...], gn_ref[...]).astype(BF16)
    y = jnp.dot(h, win_ref[...], preferred_element_type=F32)
    cos, sup, sdn = cos_ref[...], sup_ref[...], sdn_ref[...]
    low = lax.broadcasted_iota(jnp.int32, (tm, LANES), 1) < MLA_ROPE
    scale = (MLA_NOPE + MLA_ROPE) ** -0.5
    half = MLA_ROPE // 2

    cqn = _rms_rows(y[:, :MLA_Q_RANK], qag_ref[...]).astype(BF16)
    q = jnp.dot(cqn, wuq_ref[...], preferred_element_type=F32)
    ckv = _rms_rows(y[:, MLA_Q_RANK:MLA_Q_RANK + MLA_KV_RANK], kvag_ref[...])
    ckv_ref[...] = ckv
    krp = _rope(_group_rms(y[:, MLA_Q_RANK + MLA_KV_RANK:], gkr_ref[...], MLA_ROPE), cos, sup, sdn, half)
    kr_ref[...] = krp[:, :MLA_ROPE]

    gqn, gqr, kg = gqn_ref[...], gqr_ref[...], kg_ref[...]
    if absorb:
        (krp_ref,) = rest
        krp_ref[...] = krp
    else:
        kcat_ref, v_ref = rest
        kv = jnp.dot(ckv.astype(BF16), wk_ref[...], preferred_element_type=F32)
        v_ref[...] = kv[:, MLA_HEADS * MLA_NOPE:].astype(BF16)
        krb = krp.astype(BF16)

    rope_off = MLA_HEADS * MLA_NOPE
    for hd in range(MLA_HEADS):
        qn = _group_rms(q[:, hd * LANES:(hd + 1) * LANES], gqn, LANES) * scale
        if absorb:
            qn = jnp.dot((qn * kg).astype(BF16), wk_ref[hd], preferred_element_type=F32)
        q_ref[:, hd * 2 * LANES:hd * 2 * LANES + LANES] = qn.astype(q_ref.dtype)
        if not absorb:
            kn = _group_rms(kv[:, hd * LANES:(hd + 1) * LANES], kg, LANES)
            kcat_ref[:, hd * 2 * LANES:hd * 2 * LANES + LANES] = kn.astype(BF16)
            kcat_ref[:, hd * 2 * LANES + LANES:(hd + 1) * 2 * LANES] = krb
    for c in range(MLA_HEADS // 2):
        qr = _rope(_group_rms(q[:, rope_off + c * LANES:rope_off + (c + 1) * LANES], gqr, MLA_ROPE),
                   cos, sup, sdn, half) * scale
        for par in range(2):
            hd = 2 * c + par
            src = qr if par == 0 else pltpu.roll(qr, MLA_ROPE, axis=1)
            q_ref[:, hd * 2 * LANES + LANES:(hd + 1) * 2 * LANES] = jnp.where(low, src, 0.0).astype(q_ref.dtype)


def _proj_mla(x, gn, w_in, qag, kvag, w_uq, gqn, gqr, gkr, kg, tables, wk, absorb):
    n = x.shape[0]
    tm = TM_PROJ
    cos, sup, sdn = tables
    t_blocks = cos.shape[0] // tm
    row = lambda i: (i, 0)
    const = lambda i: (0, 0)
    tab = lambda i: (i % t_blocks, 0)
    wk_spec = (pl.BlockSpec(wk.shape, lambda i: (0, 0, 0)) if absorb else pl.BlockSpec(wk.shape, const))
    nq = MLA_HEADS * 2 * LANES
    out_specs = [pl.BlockSpec((tm, nq), row), pl.BlockSpec((tm, MLA_KV_RANK), row),
                 pl.BlockSpec((tm, MLA_ROPE), row)]
    out_shape = [jax.ShapeDtypeStruct((n, nq), F32 if absorb else BF16),
                 jax.ShapeDtypeStruct((n, MLA_KV_RANK), F32), jax.ShapeDtypeStruct((n, MLA_ROPE), F32)]
    if absorb:
        out_specs.append(pl.BlockSpec((tm, LANES), row))
        out_shape.append(jax.ShapeDtypeStruct((n, LANES), F32))
    else:
        out_specs += [pl.BlockSpec((tm, nq), row), pl.BlockSpec((tm, MLA_HEADS * MLA_V), row)]
        out_shape += [jax.ShapeDtypeStruct((n, nq), BF16), jax.ShapeDtypeStruct((n, MLA_HEADS * MLA_V), BF16)]
    vec = lambda w: pl.BlockSpec((1, w), const)
    return pl.pallas_call(
        functools.partial(_proj_mla_kernel, absorb=absorb),
        grid=(n // tm,),
        in_specs=[pl.BlockSpec((tm, D_MODEL), row), vec(D_MODEL), pl.BlockSpec(w_in.shape, const),
                  vec(MLA_Q_RANK), vec(MLA_KV_RANK), pl.BlockSpec(w_uq.shape, const),
                  vec(LANES), vec(LANES), vec(LANES), vec(LANES),
                  pl.BlockSpec((tm, LANES), tab), pl.BlockSpec((tm, LANES), tab),
                  pl.BlockSpec((tm, LANES), tab), wk_spec],
        out_specs=out_specs,
        out_shape=out_shape,
        compiler_params=_params("parallel"),
        name="proj_mla_absorb" if absorb else "proj_mla",
    )(x, gn, w_in, qag, kvag, w_uq, gqn, gqr, gkr, kg, cos, sup, sdn, wk)


def _da_lambda(lam_ref, lam_init):
    lp = lam_ref[...]
    a = jnp.sum(lp[0:1] * lp[1:2], axis=-1, keepdims=True)
    b = jnp.sum(lp[2:3] * lp[3:4], axis=-1, keepdims=True)
    return jnp.exp(a) - jnp.exp(b) + lam_init


def _flash_kernel(qi_ref, kj_ref, q_ref, k_ref, v_ref, *rest, n_maps, groups, kd, lam_init):
    if n_maps == 2:
        lam_ref, subg_ref, o_ref, qs_ref, m_ref, l_ref, acc_ref = rest
    else:
        o_ref, qs_ref, m_ref, l_ref, acc_ref = rest
    tq = q_ref.shape[1]
    tk = k_ref.shape[1]
    rows = groups * tq
    pair = pl.program_id(2)
    qi = qi_ref[pair]
    kj = kj_ref[pair]

    @pl.when(kj == 0)
    def _init():
        for mp in range(n_maps):
            parts = [q_ref[0, :, (mp * groups + g) * kd:(mp * groups + g + 1) * kd] for g in range(groups)]
            qs_ref[mp] = parts[0] if groups == 1 else jnp.concatenate(parts, axis=0)
        m_ref[...] = jnp.full(m_ref.shape, MASKED, F32)
        l_ref[...] = jnp.zeros(l_ref.shape, F32)
        acc_ref[...] = jnp.zeros(acc_ref.shape, F32)

    def step(diagonal):
        k = k_ref[0]
        v = v_ref[0]
        for mp in range(n_maps):
            s = lax.dot_general(qs_ref[mp], k, _NT, preferred_element_type=F32)
            if diagonal:
                r = lax.broadcasted_iota(jnp.int32, (rows, tk), 0)
                if groups > 1:
                    r = jnp.bitwise_and(r, tq - 1)
                c = lax.broadcasted_iota(jnp.int32, (rows, tk), 1)
                s = jnp.where(c <= r, s, MASKED)
            p, corr, m_new, l_new = _online_update(s, m_ref[mp], l_ref[mp])
            acc_ref[mp] = acc_ref[mp] * corr + jnp.dot(p.astype(BF16), v, preferred_element_type=F32)
            m_ref[mp] = m_new
            l_ref[mp] = l_new

    @pl.when(kj < qi)
    def _full():
        step(False)

    @pl.when(kj == qi)
    def _diag():
        step(True)
        if n_maps == 2:
            o0 = acc_ref[0] / l_ref[0]
            o1 = acc_ref[1] / l_ref[1]
            d = o0 - _da_lambda(lam_ref, lam_init) * o1
            y = _rms_rows(d, subg_ref[...]) * (1.0 - lam_init)
            for g in range(groups):
                o_ref[0, :, g * LANES:(g + 1) * LANES] = y[g * tq:(g + 1) * tq].astype(o_ref.dtype)
        else:
            o_ref[0] = (acc_ref[0] / l_ref[0]).astype(o_ref.dtype)


def _flash(q, k, v, heads, n_maps, groups, kd, extra=(), lam_init=0.0):
    b, s, _ = q.shape
    tq = TQ
    assert s % tq == 0 and tq & (tq - 1) == 0
    nt = s // tq
    qi = jnp.asarray([i for i in range(nt) for _ in range(i + 1)], jnp.int32)
    kj = jnp.asarray([j for i in range(nt) for j in range(i + 1)], jnp.int32)
    wq = n_maps * groups * kd
    wo = groups * LANES
    rows = groups * tq
    in_specs = [pl.BlockSpec((1, tq, wq), lambda bi, h, p, qi_r, kj_r: (bi, qi_r[p], h)),
                pl.BlockSpec((1, tq, kd), lambda bi, h, p, qi_r, kj_r: (bi, kj_r[p], h)),
                pl.BlockSpec((1, tq, LANES), lambda bi, h, p, qi_r, kj_r: (bi, kj_r[p], h))]
    for e in extra:
        in_specs.append(pl.BlockSpec(e.shape, lambda bi, h, p, qi_r, kj_r: (0, 0)))
    grid_spec = pltpu.PrefetchScalarGridSpec(
        num_scalar_prefetch=2,
        grid=(b, heads, int(qi.shape[0])),
        in_specs=in_specs,
        out_specs=pl.BlockSpec((1, tq, wo), lambda bi, h, p, qi_r, kj_r: (bi, qi_r[p], h)),
        scratch_shapes=[pltpu.VMEM((n_maps, rows, kd), BF16), pltpu.VMEM((n_maps, rows, 1), F32),
                        pltpu.VMEM((n_maps, rows, 1), F32), pltpu.VMEM((n_maps, rows, LANES), F32)],
    )
    return pl.pallas_call(
        functools.partial(_flash_kernel, n_maps=n_maps, groups=groups, kd=kd, lam_init=lam_init),
        grid_spec=grid_spec,
        out_shape=jax.ShapeDtypeStruct((b, s, heads * wo), BF16),
        compiler_params=_params("parallel", "parallel", "arbitrary"),
        name="flash_da" if n_maps == 2 else "flash_mla",
    )(qi, kj, q, k, v, *extra)


def _decode_da_kernel(pt_ref, q_ref, ks_ref, vs_ref, lam_ref, subg_ref, *rest, pages, lam_init):
    k_refs, v_refs = rest[:pages], rest[pages:2 * pages]
    o_ref, qst_ref, kb_ref, vb_ref, m_ref, l_ref, acc_ref = rest[2 * pages:]
    step = pl.program_id(1)
    t_new = q_ref.shape[1]
    hrows = 4 * t_new

    def attend(scores, values, mask):
        s = jnp.concatenate([scores(h) for h in range(DA_KV_HEADS)], axis=0)
        if mask is not None:
            s = jnp.where(mask, s, MASKED)
        p, corr, m_new, l_new = _online_update(s, m_ref[...], l_ref[...])
        for h in range(DA_KV_HEADS):
            rs = slice(h * hrows, (h + 1) * hrows)
            acc_ref[rs] = acc_ref[rs] * corr[rs] + jnp.dot(p[rs].astype(BF16), values(h),
                                                          preferred_element_type=F32)
        m_ref[...] = m_new
        l_ref[...] = l_new

    @pl.when(step == 0)
    def _init():
        q = q_ref[0]
        for h in range(DA_KV_HEADS):
            parts = [q[:, ((h * 2 + mp) * 2 + g) * LANES:((h * 2 + mp) * 2 + g + 1) * LANES]
                     for mp in range(2) for g in range(2)]
            qst_ref[h] = jnp.concatenate(parts, axis=0).astype(BF16)
        m_ref[...] = jnp.full(m_ref.shape, MASKED, F32)
        l_ref[...] = jnp.zeros(l_ref.shape, F32)
        acc_ref[...] = jnp.zeros(acc_ref.shape, F32)
        pad = jnp.zeros((PAGE - t_new, DA_KV_HEADS * LANES), F32)
        k_own = jnp.concatenate([ks_ref[0], pad], axis=0).astype(BF16)
        v_own = jnp.concatenate([vs_ref[0], pad], axis=0).astype(BF16)
        r = lax.broadcasted_iota(jnp.int32, (DA_KV_HEADS * hrows, PAGE), 0)
        c = lax.broadcasted_iota(jnp.int32, (DA_KV_HEADS * hrows, PAGE), 1)
        attend(lambda h: lax.dot_general(qst_ref[h], k_own[:, h * LANES:(h + 1) * LANES], _NT,
                                         preferred_element_type=F32),
               lambda h: v_own[:, h * LANES:(h + 1) * LANES],
               c <= jnp.bitwise_and(r, t_new - 1))

    for r in range(pages):
        kb_ref[:, r * PAGE:(r + 1) * PAGE] = k_refs[r][0, 0].astype(BF16)
        for h in range(DA_KV_HEADS):
            vb_ref[h, r * PAGE:(r + 1) * PAGE, :] = (
                v_refs[r][0, 0, pl.ds(h, PAGE, stride=DA_KV_HEADS), :].astype(BF16))
    attend(lambda h: jnp.dot(qst_ref[h], kb_ref[h * LANES:(h + 1) * LANES, :], preferred_element_type=F32),
           lambda h: vb_ref[h], None)

    @pl.when(step == pl.num_programs(1) - 1)
    def _finish():
        o = acc_ref[...] / l_ref[...]
        lam = _da_lambda(lam_ref, lam_init)
        for h in range(DA_KV_HEADS):
            d = o[h * hrows:h * hrows + 2 * t_new] - lam * o[h * hrows + 2 * t_new:(h + 1) * hrows]
            y = _rms_rows(d, subg_ref[...]) * (1.0 - lam_init)
            for g in range(2):
                o_ref[0, :, (h * 2 + g) * LANES:(h * 2 + g + 1) * LANES] = y[g * t_new:(g + 1) * t_new]


def _decode_da(page_table, q, k_own, v_own, lam_p, subg, cache_kt, cache_v, layer, lam_init):
    nb, t_new, wq = q.shape
    n_pages = page_table.shape[1]
    pages = DA_PAGES_PER_STEP
    assert n_pages % pages == 0 and t_new == SUBLANES
    wkv = DA_KV_HEADS * LANES

    def page_spec(r):
        return pl.BlockSpec((1, 1, wkv, PAGE),
                            lambda b, j, pt: (layer, pt[b * n_pages + j * pages + r], 0, 0))

    per_seq = lambda w: pl.BlockSpec((1, t_new, w), lambda b, j, pt: (b, 0, 0))
    const = lambda a: pl.BlockSpec(a.shape, lambda b, j, pt: (0, 0))
    rows = DA_KV_HEADS * 4 * t_new
    grid_spec = pltpu.PrefetchScalarGridSpec(
        num_scalar_prefetch=1,
        grid=(nb, n_pages // pages),
        in_specs=([per_seq(wq), per_seq(wkv), per_seq(wkv), const(lam_p), const(subg)]
                  + [page_spec(r) for r in range(pages)] * 2),
        out_specs=per_seq(2 * wkv),
        scratch_shapes=[pltpu.VMEM((DA_KV_HEADS, 4 * t_new, LANES), BF16),
                        pltpu.VMEM((wkv, pages * PAGE), BF16),
                        pltpu.VMEM((DA_KV_HEADS, pages * PAGE, LANES), BF16),
                        pltpu.VMEM((rows, 1), F32), pltpu.VMEM((rows, 1), F32),
                        pltpu.VMEM((rows, LANES), F32)],
    )
    return pl.pallas_call(
        functools.partial(_decode_da_kernel, pages=pages, lam_init=lam_init),
        grid_spec=grid_spec,
        out_shape=jax.ShapeDtypeStruct((nb, t_new, 2 * wkv), F32),
        compiler_params=_params("parallel", "arbitrary"),
        name="decode_da",
    )(page_table.reshape(-1), q, k_own, v_own, lam_p, subg,
      *([cache_kt] * pages), *([cache_v] * pages))


def _decode_mla_kernel(pt_ref, q_ref, cs_ref, rs_ref, wukt_ref, wuv_ref, *rest, pages, chunk):
    c_refs, r_refs = rest[:pages], rest[pages:2 * pages]
    o_ref, lhs_ref, qr_ref, cc_ref, rb_ref, m_ref, l_ref, acc_ref = rest[2 * pages:]
    step = pl.program_id(1)
    t_new = q_ref.shape[1]
    nq = MLA_HEADS * t_new
    ne = MLA_HEADS * MLA_NOPE

    def attend(tok, bp, mask):
        res = lax.dot_general(lhs_ref[...], tok, _NT, preferred_element_type=F32)
        e2 = res[:ne] * res[:ne]
        a = res[ne:]
        s_parts = []
        for h in range(MLA_HEADS):
            ms = jnp.sum(e2[h * MLA_NOPE:(h + 1) * MLA_NOPE], axis=0, keepdims=True) * (1.0 / MLA_NOPE)
            s_parts.append(a[h * t_new:(h + 1) * t_new] * lax.rsqrt(ms + EPS)
                           + bp[h * t_new:(h + 1) * t_new])
        s = jnp.concatenate(s_parts, axis=0)
        if mask is not None:
            s = jnp.where(mask, s, MASKED)
        p, corr, m_new, l_new = _online_update(s, m_ref[...], l_ref[...])
        acc_ref[...] = acc_ref[...] * corr + jnp.dot(p.astype(BF16), tok, preferred_element_type=F32)
        m_ref[...] = m_new
        l_ref[...] = l_new

    @pl.when(step == 0)
    def _init():
        q = q_ref[0]
        qa = jnp.concatenate([q[:, h * 2 * LANES:h * 2 * LANES + LANES] for h in range(MLA_HEADS)], axis=0)
        qr = jnp.concatenate([q[:, h * 2 * LANES + LANES:(h + 1) * 2 * LANES] for h in range(MLA_HEADS)], axis=0)
        lhs_ref[0:ne, :] = wukt_ref[...]
        lhs_ref[ne:ne + nq, :] = qa.astype(BF16)
        qr_ref[...] = qr.astype(BF16)
        rb_ref[MLA_ROPE:, :] = jnp.zeros((LANES - MLA_ROPE, rb_ref.shape[1]), BF16)
        m_ref[...] = jnp.full(m_ref.shape, MASKED, F32)
        l_ref[...] = jnp.zeros(l_ref.shape, F32)
        acc_ref[...] = jnp.zeros(acc_ref.shape, F32)
        pad = jnp.zeros((PAGE - t_new, LANES), F32)
        c_own = jnp.concatenate([cs_ref[0], pad], axis=0).astype(BF16)
        r_own = jnp.concatenate([rs_ref[0], pad], axis=0).astype(BF16)
        r = lax.broadcasted_iota(jnp.int32, (nq, PAGE), 0)
        c = lax.broadcasted_iota(jnp.int32, (nq, PAGE), 1)
        attend(c_own, lax.dot_general(qr_ref[...], r_own, _NT, preferred_element_type=F32),
               c <= jnp.bitwise_and(r, t_new - 1))

    for r in range(pages):
        cc_ref[r * PAGE:(r + 1) * PAGE, :] = c_refs[r][0, 0].astype(BF16)
        rb_ref[0:MLA_ROPE, r * PAGE:(r + 1) * PAGE] = r_refs[r][0, 0].astype(BF16)
    for ck in range(pages * PAGE // chunk):
        cs = slice(ck * chunk, (ck + 1) * chunk)
        attend(cc_ref[cs, :], jnp.dot(qr_ref[...], rb_ref[:, cs], preferred_element_type=F32), None)

    @pl.when(step == pl.num_programs(1) - 1)
    def _finish():
        o_lat = (acc_ref[...] / l_ref[...]).astype(BF16)
        o_all = jnp.dot(o_lat, wuv_ref[...], preferred_element_type=F32)
        for h in range(MLA_HEADS):
            o_ref[0, :, h * MLA_V:(h + 1) * MLA_V] = o_all[h * t_new:(h + 1) * t_new,
                                                           h * MLA_V:(h + 1) * MLA_V]


def _decode_mla(page_table, q, c_own, r_own, wukt, wuv, cache_c, cache_rt, layer):
    nb, t_new, wq = q.shape
    n_pages = page_table.shape[1]
    pages = MLA_PAGES_PER_STEP
    chunk = MLA_CHUNK
    assert n_pages % pages == 0 and (pages * PAGE) % chunk == 0 and t_new == SUBLANES

    def page_spec(r, rows):
        return pl.BlockSpec((1, 1, rows, PAGE),
                            lambda b, j, pt: (layer, pt[b * n_pages + j * pages + r], 0, 0))

    per_seq = lambda w: pl.BlockSpec((1, t_new, w), lambda b, j, pt: (b, 0, 0))
    const = lambda a: pl.BlockSpec(a.shape, lambda b, j, pt: (0, 0))
    nq = MLA_HEADS * t_new
    grid_spec = pltpu.PrefetchScalarGridSpec(
        num_scalar_prefetch=1,
        grid=(nb, n_pages // pages),
        in_specs=([per_seq(wq), per_seq(MLA_KV_RANK), per_seq(LANES), const(wukt), const(wuv)]
                  + [page_spec(r, PAGE) for r in range(pages)]
                  + [page_spec(r, MLA_ROPE) for r in range(pages)]),
        out_specs=per_seq(MLA_HEADS * MLA_V),
        scratch_shapes=[pltpu.VMEM((MLA_HEADS * MLA_NOPE + nq, MLA_KV_RANK), BF16),
                        pltpu.VMEM((nq, LANES), BF16),
                        pltpu.VMEM((pages * PAGE, MLA_KV_RANK), BF16),
                        pltpu.VMEM((LANES, pages * PAGE), BF16),
                        pltpu.VMEM((nq, 1), F32), pltpu.VMEM((nq, 1), F32),
                        pltpu.VMEM((nq, MLA_KV_RANK), F32)],
    )
    return pl.pallas_call(
        functools.partial(_decode_mla_kernel, pages=pages, chunk=chunk),
        grid_spec=grid_spec,
        out_shape=jax.ShapeDtypeStruct((nb, t_new, MLA_HEADS * MLA_V), F32),
        compiler_params=_params("parallel", "arbitrary"),
        name="decode_mla",
    )(page_table.reshape(-1), q, c_own, r_own, wukt, wuv,
      *([cache_c] * pages), *([cache_rt] * pages))


def _ffn_kernel(x_ref, a_ref, wo_ref, gn_ref, wg_ref, wu_ref, cw_ref, cb_ref, wd_ref, prev_ref,
                xo_ref, go_ref, h_ref, *carry, tiles_per_seq):
    i = pl.program_id(0)
    j = pl.program_id(1)
    tm = x_ref.shape[0]
    tf = wg_ref.shape[1]

    @pl.when(j == 0)
    def _prologue():
        x1 = x_ref[...] + jnp.dot(a_ref[...].astype(BF16), wo_ref[...], preferred_element_type=F32)
        xo_ref[...] = x1
        h_ref[...] = _rms_rows(x1, gn_ref[...]).astype(BF16)

    h = h_ref[...]
    g = jnp.dot(h, wg_ref[...], preferred_element_type=F32)
    u = jnp.dot(h, wu_ref[...], preferred_element_type=F32)
    row = lax.broadcasted_iota(jnp.int32, (tm, tf), 0)
    if tiles_per_seq:
        (carry_ref,) = carry

        @pl.when(i % tiles_per_seq == 0)
        def _seq_start():
            carry_ref[j, SUBLANES - 2:SUBLANES, :] = prev_ref[0]

        tail = carry_ref[j]
        p0, p1 = tail[SUBLANES - 2:SUBLANES - 1], tail[SUBLANES - 1:SUBLANES]
        carry_ref[j] = g[tm - SUBLANES:]
        go_ref[0] = g[tm - SUBLANES:]
        t = row
    else:
        nseq = tm // SUBLANES
        prev = prev_ref[...]
        spread = lambda v: jnp.broadcast_to(v, (nseq, SUBLANES, tf)).reshape(tm, tf)
        p0, p1 = spread(prev[:, 0:1, :]), spread(prev[:, 1:2, :])
        go_ref[...] = g
        t = jnp.bitwise_and(row, SUBLANES - 1)
    g1 = jnp.where(t == 0, p1, pltpu.roll(g, 1, axis=0))
    g2 = jnp.where(t == 0, p0, jnp.where(t == 1, p1, pltpu.roll(g, 2, axis=0)))
    cw = cw_ref[...]
    gc = cb_ref[...] + cw[0:1] * g2 + cw[1:2] * g1 + cw[2:3] * g
    act = gc * (1.0 / (1.0 + jnp.exp(-gc))) * u
    xo_ref[...] += jnp.dot(act.astype(BF16), wd_ref[...], preferred_element_type=F32)


def _ffn(x, attn, w_out, gn, w_gu, conv_w, conv_b, w_down, prev, tm, tiles_per_seq):
    n = x.shape[0]
    tf = TF
    nf = D_FF // tf
    nt = n // tm
    row = lambda i, j: (i, 0)
    const = lambda i, j: (0, 0)
    if tiles_per_seq:
        prev_spec = pl.BlockSpec((1, 2, tf), lambda i, j: (i // tiles_per_seq, 0, j))
        go_spec = pl.BlockSpec((1, SUBLANES, tf), lambda i, j: (i, 0, j))
        go_shape = jax.ShapeDtypeStruct((nt, SUBLANES, D_FF), F32)
        scratch = [pltpu.VMEM((tm, D_MODEL), BF16), pltpu.VMEM((nf, SUBLANES, tf), F32)]
    else:
        prev_spec = pl.BlockSpec((tm // SUBLANES, 2, tf), lambda i, j: (i, 0, j))
        go_spec = pl.BlockSpec((tm, tf), lambda i, j: (i, j))
        go_shape = jax.ShapeDtypeStruct((n, D_FF), F32)
        scratch = [pltpu.VMEM((tm, D_MODEL), BF16)]
    return pl.pallas_call(
        functools.partial(_ffn_kernel, tiles_per_seq=tiles_per_seq),
        grid=(nt, nf),
        in_specs=[pl.BlockSpec((tm, D_MODEL), row), pl.BlockSpec((tm, D_MODEL), row),
                  pl.BlockSpec((D_MODEL, D_MODEL), const), pl.BlockSpec((1, D_MODEL), const),
                  pl.BlockSpec((D_MODEL, tf), lambda i, j: (0, j)),
                  pl.BlockSpec((D_MODEL, tf), lambda i, j: (0, nf + j)),
                  pl.BlockSpec((3, tf), lambda i, j: (0, j)), pl.BlockSpec((1, tf), lambda i, j: (0, j)),
                  pl.BlockSpec((tf, D_MODEL), lambda i, j: (j, 0)), prev_spec],
        out_specs=[pl.BlockSpec((tm, D_MODEL), row), go_spec],
        out_shape=[jax.ShapeDtypeStruct((n, D_MODEL), F32), go_shape],
        scratch_shapes=scratch,
        compiler_params=_params("arbitrary", "arbitrary"),
        name="conv_ffn_seq" if tiles_per_seq else "conv_ffn_blocks",
    )(x, attn, w_out, gn, w_gu, w_gu, conv_w, conv_b, w_down, prev)


def _rope_tables(pos, rot, theta, rows):
    half = rot // 2
    inv = 1.0 / (theta ** (jnp.arange(half, dtype=F32) / half))
    ang = pos.astype(F32)[:, None] * inv[None, :]
    d = jnp.arange(LANES) % 64
    idx = d % half
    cos = jnp.where(d < rot, jnp.cos(ang)[:, idx], 1.0)
    sin = jnp.sin(ang)[:, idx]
    sin_up = jnp.where(d < half, -sin, 0.0)
    sin_dn = jnp.where((d >= half) & (d < rot), sin, 0.0)
    reps = rows // pos.shape[0] if pos.shape[0] < rows else 1
    return tuple(jnp.tile(t.astype(F32), (reps, 1)) for t in (cos, sin_up, sin_dn))


def _row(v):
    return v.reshape(1, -1).astype(F32)


def _tile2(v):
    return jnp.tile(v.astype(F32), 2).reshape(1, LANES)


def kernel(x_prompt, x_sample, cache_da_k, cache_da_v, cache_mla_ckv, cache_mla_kpe, state_conv, page_table,
           attn_norm, ffn_norm, da_w_in, da_qk_norm, da_lambda, da_sub_norm, da_w_out,
           mla_w_in, mla_q_a_norm, mla_kv_a_norm, mla_w_uq, mla_w_ukv, mla_qk_norm, mla_w_out,
           ffn_w_gu, ffn_conv_w, ffn_conv_b, ffn_w_down):
    b, s, d = x_prompt.shape
    nb, t_new, _ = x_sample.shape
    n_pages = page_table.shape[1]
    depth = ffn_w_gu.shape[0]
    n_pool = cache_da_k.shape[1]
    assert d == D_MODEL and s % TM_FFN_PROMPT == 0 and (nb * t_new) % TM_PROJ == 0

    pos_p = jnp.arange(s)
    pos_s = n_pages * PAGE + jnp.arange(t_new)
    da_tab_p = _rope_tables(pos_p, 2 * DA_ROT_HALF, DA_THETA, TM_PROJ)
    da_tab_s = _rope_tables(pos_s, 2 * DA_ROT_HALF, DA_THETA, TM_PROJ)
    mla_tab_p = _rope_tables(pos_p, MLA_ROPE, MLA_THETA, TM_PROJ)
    mla_tab_s = _rope_tables(pos_s, MLA_ROPE, MLA_THETA, TM_PROJ)

    ck = jnp.transpose(cache_da_k, (0, 1, 3, 4, 5, 2)).reshape(-1, n_pool, DA_KV_HEADS * LANES, PAGE)
    cv = cache_da_v.reshape(-1, n_pool, PAGE * DA_KV_HEADS, LANES)
    cr = jnp.transpose(cache_mla_kpe, (0, 1, 3, 2))
    zero_prev = jnp.zeros((b, 2, D_FF), F32)

    xp = x_prompt.reshape(b * s, d)
    xs = x_sample.reshape(nb * t_new, d)
    out = {name: [] for name in ("dkp", "dvp", "ckp", "kep", "cvp", "dks", "dvs", "cks", "kes", "cvs")}
    for i in range(depth):
        li = i // N_MIXERS
        gn = _row(attn_norm[i])
        if i % N_MIXERS == 0:
            lam_init = 0.8 - 0.6 * math.exp(-0.3 * i)
            w_in = da_w_in[li].astype(BF16)
            gq, gk = _tile2(da_qk_norm[li, 0]), _tile2(da_qk_norm[li, 1])
            lam_p, subg = da_lambda[li].astype(F32), _row(da_sub_norm[li])
            w_out = da_w_out[li].astype(BF16)
            qp, kp, vp, kpb, vpb = _proj_da(xp, gn, w_in, gq, gk, da_tab_p, BF16)
            ap = _flash(qp.reshape(b, s, -1), kpb.reshape(b, s, -1), vpb.reshape(b, s, -1),
                        DA_KV_HEADS, 2, 2, LANES, extra=(lam_p, subg), lam_init=lam_init)
            qs, ks, vs, _, _ = _proj_da(xs, gn, w_in, gq, gk, da_tab_s, F32)
            a_s = _decode_da(page_table, qs.reshape(nb, t_new, -1), ks.reshape(nb, t_new, -1),
                             vs.reshape(nb, t_new, -1), lam_p, subg, ck, cv, li, lam_init)
            out["dkp"].append(kp.reshape(b, s, DA_KV_HEADS, 2, DA_HEAD_DIM))
            out["dvp"].append(vp.reshape(b, s, DA_KV_HEADS, 2 * DA_HEAD_DIM))
            out["dks"].append(ks.reshape(nb, t_new, DA_KV_HEADS, 2, DA_HEAD_DIM))
            out["dvs"].append(vs.reshape(nb, t_new, DA_KV_HEADS, 2 * DA_HEAD_DIM))
        else:
            w_in = jnp.pad(mla_w_in[li], ((0, 0), (0, 4 * LANES - mla_w_in.shape[2]))).astype(BF16)
            w_uq = mla_w_uq[li].reshape(MLA_Q_RANK, MLA_HEADS, MLA_NOPE + MLA_ROPE)
            w_uq = jnp.concatenate([w_uq[:, :, :MLA_NOPE].reshape(MLA_Q_RANK, -1),
                                    w_uq[:, :, MLA_NOPE:].reshape(MLA_Q_RANK, -1)], axis=1).astype(BF16)
            w_uk = mla_w_ukv[li, :, :, :MLA_NOPE]
            w_uv = mla_w_ukv[li, :, :, MLA_NOPE:].reshape(MLA_KV_RANK, -1)
            w_ukv = jnp.concatenate([w_uk.reshape(MLA_KV_RANK, -1), w_uv], axis=1).astype(BF16)
            w_ukt = jnp.transpose(w_uk, (1, 2, 0)).astype(BF16)
            qag, kvag = _row(mla_q_a_norm[li]), _row(mla_kv_a_norm[li])
            gqn, kg = _row(mla_qk_norm[li, 0, :MLA_NOPE]), _row(mla_qk_norm[li, 1, :MLA_NOPE])
            gqr, gkr = _tile2(mla_qk_norm[li, 0, MLA_NOPE:]), _tile2(mla_qk_norm[li, 1, MLA_NOPE:])
            w_out = mla_w_out[li].astype(BF16)
            qp, ckvp, krp, kcat, vpb = _proj_mla(xp, gn, w_in, qag, kvag, w_uq, gqn, gqr, gkr, kg,
                                                 mla_tab_p, w_ukv, absorb=False)
            ap = _flash(qp.reshape(b, s, -1), kcat.reshape(b, s, -1), vpb.reshape(b, s, -1),
                        MLA_HEADS, 1, 1, 2 * LANES)
            qs, ckvs, krs, krs_pad = _proj_mla(xs, gn, w_in, qag, kvag, w_uq, gqn, gqr, gkr, kg,
                                               mla_tab_s, w_ukt, absorb=True)
            a_s = _decode_mla(page_table, qs.reshape(nb, t_new, -1), ckvs.reshape(nb, t_new, -1),
                              krs_pad.reshape(nb, t_new, -1), w_ukt.reshape(-1, MLA_KV_RANK),
                              w_uv.astype(BF16), cache_mla_ckv, cr, li)
            out["ckp"].append(ckvp.reshape(b, s, MLA_KV_RANK))
            out["kep"].append(krp.reshape(b, s, MLA_ROPE))
            out["cks"].append(ckvs.reshape(nb, t_new, MLA_KV_RANK))
            out["kes"].append(krs.reshape(nb, t_new, MLA_ROPE))
        fgn = _row(ffn_norm[i])
        w_gu, w_down = ffn_w_gu[i].astype(BF16), ffn_w_down[i].astype(BF16)
        cw, cb = ffn_conv_w[i].astype(F32), _row(ffn_conv_b[i])
        tps = s // TM_FFN_PROMPT
        xp, gp = _ffn(xp, ap.reshape(b * s, d), w_out, fgn, w_gu, cw, cb, w_down, zero_prev,
                      TM_FFN_PROMPT, tps)
        xs, gs = _ffn(xs, a_s.reshape(nb * t_new, d), w_out, fgn, w_gu, cw, cb, w_down, state_conv[i],
                      TM_FFN_SAMPLE, 0)
        out["cvp"].append(gp.reshape(b, tps, SUBLANES, D_FF)[:, -1, SUBLANES - 2:, :])
        out["cvs"].append(gs.reshape(nb, t_new, D_FF)[:, t_new - 2:, :])
    st = jnp.stack
    return (xp.reshape(b, s, d), xs.reshape(nb, t_new, d), st(out["dkp"]), st(out["dvp"]), st(out["ckp"]),
            st(out["kep"]), st(out["cvp"]), st(out["dks"]), st(out["dvs"]), st(out["cks"]), st(out["kes"]),
            st(out["cvs"]))
```

```python
import functools
import math

import jax
import jax.numpy as jnp
from jax import lax
from jax.experimental import pallas as pl
from jax.experimental.pallas import tpu as pltpu

F32 = jnp.float32
BF16 = jnp.bfloat16

EPS = 1e-6
LANES = 128
SUBLANES = 8
PAGE = 128
MASKED = -1e30
VMEM_LIMIT = 56 * 1024 * 1024

D_MODEL = 1024
DA_KV_HEADS = 4
DA_HEAD_DIM = 64
DA_ROT_HALF = 8
DA_THETA = 500000.0
MLA_HEADS = 8
MLA_Q_RANK = 256
MLA_KV_RANK = 128
MLA_NOPE = 128
MLA_ROPE = 64
MLA_V = 128
MLA_THETA = 10000.0
D_FF = 2816
N_MIXERS = 2

TM_PROJ = 512
TQ = 1024
FLASH_MLA_HEADS_PER_STEP = 2
FLASH_ROW_BLOCK = 256
TM_FFN_PROMPT = 1024
TM_FFN_SAMPLE = 256
TF = 256
DA_PAGES_PER_STEP = 32
MLA_PAGES_PER_STEP = 16
MLA_CHUNK = 256

_NT = (((1,), (1,)), ((), ()))


def _params(*semantics):
    return pltpu.CompilerParams(dimension_semantics=semantics, vmem_limit_bytes=VMEM_LIMIT)


def _rms_rows(x, g):
    return x * lax.rsqrt(jnp.mean(x * x, axis=-1, keepdims=True) + EPS) * g


def _group_sumsq(x, group):
    x2 = x * x
    if group == LANES:
        return jnp.sum(x2, axis=-1, keepdims=True)
    shift = int(math.log2(group))
    r = lax.shift_right_logical(lax.broadcasted_iota(jnp.int32, (LANES, LANES), 0), shift)
    c = lax.shift_right_logical(lax.broadcasted_iota(jnp.int32, (LANES, LANES), 1), shift)
    ones = jnp.where(r == c, 1.0, 0.0).astype(BF16)
    hi = x2.astype(BF16)
    lo = (x2 - hi.astype(F32)).astype(BF16)
    return (jnp.dot(hi, ones, preferred_element_type=F32)
            + jnp.dot(lo, ones, preferred_element_type=F32))


def _group_rms(x, g, group):
    return x * lax.rsqrt(_group_sumsq(x, group) * (1.0 / group) + EPS) * g


def _rope(x, cos, sin_up, sin_dn, half):
    x_up = pltpu.roll(x, LANES - half, axis=1)
    x_dn = pltpu.roll(x, half, axis=1)
    return x * cos + x_up * sin_up + x_dn * sin_dn


def _online_update(s, m_prev, l_prev):
    m_new = jnp.maximum(m_prev, jnp.max(s, axis=-1, keepdims=True))
    corr = jnp.exp(m_prev - m_new)
    p = jnp.exp(s - m_new)
    l_new = l_prev * corr + jnp.sum(p, axis=-1, keepdims=True)
    return p, corr, m_new, l_new


def _proj_da_kernel(x_ref, gn_ref, w_ref, gq_ref, gk_ref, cos_ref, sup_ref, sdn_ref,
                    q_ref, k_ref, v_ref, kb_ref, vb_ref):
    tm = x_ref.shape[0]
    h = _rms_rows(x_ref[...], gn_ref[...]).astype(BF16)
    y = jnp.dot(h, w_ref[...], preferred_element_type=F32)
    cos, sup, sdn = cos_ref[...], sup_ref[...], sdn_ref[...]
    low = lax.broadcasted_iota(jnp.int32, (tm, LANES), 1) < DA_HEAD_DIM
    scale = DA_HEAD_DIM ** -0.5

    def norm_rope(blk, g):
        return _rope(_group_rms(blk, g, DA_HEAD_DIM), cos, sup, sdn, DA_ROT_HALF)

    gq, gk = gq_ref[...], gk_ref[...]
    for c in range(2 * DA_KV_HEADS):
        qb = norm_rope(y[:, c * LANES:(c + 1) * LANES], gq) * scale
        kvh, grp = divmod(c, 2)
        for mp in range(2):
            oc = (kvh * 2 + mp) * 2 + grp
            qm = jnp.where(low if mp == 0 else jnp.logical_not(low), qb, 0.0)
            q_ref[:, oc * LANES:(oc + 1) * LANES] = qm.astype(q_ref.dtype)
    k_off = 2 * DA_KV_HEADS * LANES
    for c in range(DA_KV_HEADS):
        kb = norm_rope(y[:, k_off + c * LANES:k_off + (c + 1) * LANES], gk)
        k_ref[:, c * LANES:(c + 1) * LANES] = kb
        kb_ref[:, c * LANES:(c + 1) * LANES] = kb.astype(BF16)
    v = y[:, k_off + DA_KV_HEADS * LANES:]
    v_ref[...] = v
    vb_ref[...] = v.astype(BF16)


def _proj_da(x, gn, w_in, gq, gk, tables, q_dtype):
    n = x.shape[0]
    tm = TM_PROJ
    cos, sup, sdn = tables
    t_blocks = cos.shape[0] // tm
    row = lambda i: (i, 0)
    const = lambda i: (0, 0)
    tab = lambda i: (i % t_blocks, 0)
    nq, nk = 4 * DA_KV_HEADS * 2 * LANES, DA_KV_HEADS * LANES
    return pl.pallas_call(
        _proj_da_kernel,
        grid=(n // tm,),
        in_specs=[pl.BlockSpec((tm, D_MODEL), row), pl.BlockSpec((1, D_MODEL), const),
                  pl.BlockSpec(w_in.shape, const), pl.BlockSpec((1, LANES), const),
                  pl.BlockSpec((1, LANES), const), pl.BlockSpec((tm, LANES), tab),
                  pl.BlockSpec((tm, LANES), tab), pl.BlockSpec((tm, LANES), tab)],
        out_specs=[pl.BlockSpec((tm, nq), row), pl.BlockSpec((tm, nk), row), pl.BlockSpec((tm, nk), row),
                   pl.BlockSpec((tm, nk), row), pl.BlockSpec((tm, nk), row)],
        out_shape=[jax.ShapeDtypeStruct((n, nq), q_dtype), jax.ShapeDtypeStruct((n, nk), F32),
                   jax.ShapeDtypeStruct((n, nk), F32), jax.ShapeDtypeStruct((n, nk), BF16),
                   jax.ShapeDtypeStruct((n, nk), BF16)],
        compiler_params=_params("parallel"),
        name="proj_da",
    )(x, gn, w_in, gq, gk, cos, sup, sdn)


def _proj_mla_kernel(x_ref, gn_ref, win_ref, qag_ref, kvag_ref, wuq_ref, gqn_ref, gqr_ref, gkr_ref,
                     kg_ref, cos_ref, sup_ref, sdn_ref, wk_ref, q_ref, ckv_ref, kr_ref, *rest, absorb):
    tm = x_ref.shape[0]
    h = _rms_rows(x_ref[...], gn_ref[...]).astype(BF16)
    y = jnp.dot(h, win_ref[...], preferred_element_type=F32)
    cos, sup, sdn = cos_ref[...], sup_ref[...], sdn_ref[...]
    low = lax.broadcasted_iota(jnp.int32, (tm, LANES), 1) < MLA_ROPE
    scale = (MLA_NOPE + MLA_ROPE) ** -0.5
    half = MLA_ROPE // 2

    cqn = _rms_rows(y[:, :MLA_Q_RANK], qag_ref[...]).astype(BF16)
    q = jnp.dot(cqn, wuq_ref[...], preferred_element_type=F32)
    ckv = _rms_rows(y[:, MLA_Q_RANK:MLA_Q_RANK + MLA_KV_RANK], kvag_ref[...])
    ckv_ref[...] = ckv
    krp = _rope(_group_rms(y[:, MLA_Q_RANK + MLA_KV_RANK:], gkr_ref[...], MLA_ROPE), cos, sup, sdn, half)
    kr_ref[...] = krp[:, :MLA_ROPE]

    gqn, gqr, kg = gqn_ref[...], gqr_ref[...], kg_ref[...]
    if absorb:
        (krp_ref,) = rest
        krp_ref[...] = krp
    else:
        kcat_ref, v_ref = rest
        kv = jnp.dot(ckv.astype(BF16), wk_ref[...], preferred_element_type=F32)
        v_ref[...] = kv[:, MLA_HEADS * MLA_NOPE:].astype(BF16)
        krb = krp.astype(BF16)

    rope_off = MLA_HEADS * MLA_NOPE
    for hd in range(MLA_HEADS):
        qn = _group_rms(q[:, hd * LANES:(hd + 1) * LANES], gqn, LANES) * scale
        if absorb:
            qn = jnp.dot((qn * kg).astype(BF16), wk_ref[hd], preferred_element_type=F32)
        q_ref[:, hd * 2 * LANES:hd * 2 * LANES + LANES] = qn.astype(q_ref.dtype)
        if not absorb:
            kn = _group_rms(kv[:, hd * LANES:(hd + 1) * LANES], kg, LANES)
            kcat_ref[:, hd * 2 * LANES:hd * 2 * LANES + LANES] = kn.astype(BF16)
            kcat_ref[:, hd * 2 * LANES + LANES:(hd + 1) * 2 * LANES] = krb
    for c in range(MLA_HEADS // 2):
        qr = _rope(_group_rms(q[:, rope_off + c * LANES:rope_off + (c + 1) * LANES], gqr, MLA_ROPE),
                   cos, sup, sdn, half) * scale
        for par in range(2):
            hd = 2 * c + par
            src = qr if par == 0 else pltpu.roll(qr, MLA_ROPE, axis=1)
            q_ref[:, hd * 2 * LANES + LANES:(hd + 1) * 2 * LANES] = jnp.where(low, src, 0.0).astype(q_ref.dtype)


def _proj_mla(x, gn, w_in, qag, kvag, w_uq, gqn, gqr, gkr, kg, tables, wk, absorb):
    n = x.shape[0]
    tm = TM_PROJ
    cos, sup, sdn = tables
    t_blocks = cos.shape[0] // tm
    row = lambda i: (i, 0)
    const = lambda i: (0, 0)
    tab = lambda i: (i % t_blocks, 0)
    wk_spec = (pl.BlockSpec(wk.shape, lambda i: (0, 0, 0)) if absorb else pl.BlockSpec(wk.shape, const))
    nq = MLA_HEADS * 2 * LANES
    out_specs = [pl.BlockSpec((tm, nq), row), pl.BlockSpec((tm, MLA_KV_RANK), row),
                 pl.BlockSpec((tm, MLA_ROPE), row)]
    out_shape = [jax.ShapeDtypeStruct((n, nq), F32 if absorb else BF16),
                 jax.ShapeDtypeStruct((n, MLA_KV_RANK), F32), jax.ShapeDtypeStruct((n, MLA_ROPE), F32)]
    if absorb:
        out_specs.append(pl.BlockSpec((tm, LANES), row))
        out_shape.append(jax.ShapeDtypeStruct((n, LANES), F32))
    else:
        out_specs += [pl.BlockSpec((tm, nq), row), pl.BlockSpec((tm, MLA_HEADS * MLA_V), row)]
        out_shape += [jax.ShapeDtypeStruct((n, nq), BF16), jax.ShapeDtypeStruct((n, MLA_HEADS * MLA_V), BF16)]
    vec = lambda w: pl.BlockSpec((1, w), const)
    return pl.pallas_call(
        functools.partial(_proj_mla_kernel, absorb=absorb),
        grid=(n // tm,),
        in_specs=[pl.BlockSpec((tm, D_MODEL), row), vec(D_MODEL), pl.BlockSpec(w_in.shape, const),
                  vec(MLA_Q_RANK), vec(MLA_KV_RANK), pl.BlockSpec(w_uq.shape, const),
                  vec(LANES), vec(LANES), vec(LANES), vec(LANES),
                  pl.BlockSpec((tm, LANES), tab), pl.BlockSpec((tm, LANES), tab),
                  pl.BlockSpec((tm, LANES), tab), wk_spec],
        out_specs=out_specs,
        out_shape=out_shape,
        compiler_params=_params("parallel"),
        name="proj_mla_absorb" if absorb else "proj_mla",
    )(x, gn, w_in, qag, kvag, w_uq, gqn, gqr, gkr, kg, cos, sup, sdn, wk)


def _da_lambda(lam_ref, lam_init):
    lp = lam_ref[...]
    a = jnp.sum(lp[0:1] * lp[1:2], axis=-1, keepdims=True)
    b = jnp.sum(lp[2:3] * lp[3:4], axis=-1, keepdims=True)
    return jnp.exp(a) - jnp.exp(b) + lam_init


def _flash_kernel(qi_ref, kj_ref, q_ref, k_ref, v_ref, *rest, hp, n_maps, groups, kd, lam_init, rb):
    if n_maps == 2:
        lam_ref, subg_ref, o_ref, qs_ref, va_ref, m_ref, acc_ref = rest
    else:
        o_ref, qs_ref, va_ref, m_ref, acc_ref = rest
    tq = q_ref.shape[1]
    tk = k_ref.shape[1]
    rows = groups * tq
    pair = pl.program_id(2)
    qi = qi_ref[pair]
    kj = kj_ref[pair]

    @pl.when(kj == 0)
    def _init():
        for hd in range(hp):
            for mp in range(n_maps):
                for g in range(groups):
                    col = ((hd * n_maps + mp) * groups + g) * kd
                    qs_ref[hd * n_maps + mp, g * tq:(g + 1) * tq, :] = q_ref[0, :, col:col + kd]
            va_ref[hd, :, LANES:] = jnp.ones((tk, LANES), BF16)
        m_ref[...] = jnp.full(m_ref.shape, MASKED, F32)
        acc_ref[...] = jnp.zeros(acc_ref.shape, F32)

    def step(diagonal):
        for hd in range(hp):
            va_ref[hd, :, :LANES] = v_ref[0, :, hd * LANES:(hd + 1) * LANES]
        for blk in range(rows // rb):
            lo = (blk * rb) % tq
            keys = lo + rb if diagonal else tk
            rs = slice(blk * rb, (blk + 1) * rb)
            for hm in range(hp * n_maps):
                hd = hm // n_maps
                s = lax.dot_general(qs_ref[hm, rs, :], k_ref[0, :keys, hd * kd:(hd + 1) * kd], _NT,
                                    preferred_element_type=F32)
                cols = [s[:, c * LANES:(c + 1) * LANES] for c in range(keys // LANES)]
                if diagonal:
                    r = lax.broadcasted_iota(jnp.int32, (rb, LANES), 0) + lo
                    c = lax.broadcasted_iota(jnp.int32, (rb, LANES), 1)
                    for cb in range(lo // LANES, keys // LANES):
                        cols[cb] = jnp.where(c + cb * LANES <= r, cols[cb], MASKED)
                mx = cols[0]
                for col in cols[1:]:
                    mx = jnp.maximum(mx, col)
                m_prev = m_ref[hm, rs, :]
                m_new = jnp.maximum(m_prev, jnp.max(mx, axis=-1, keepdims=True))
                corr = jnp.exp(m_prev - m_new)
                p = jnp.concatenate([jnp.exp(col - m_new).astype(BF16) for col in cols], axis=1)
                pv = jnp.dot(p, va_ref[hd, :keys, :], preferred_element_type=F32)
                acc_ref[hm, rs, :] = acc_ref[hm, rs, :] * jnp.concatenate([corr, corr], axis=1) + pv
                m_ref[hm, rs, :] = m_new

    @pl.when(kj < qi)
    def _full():
        step(False)

    @pl.when(kj == qi)
    def _diag():
        step(True)
        for hd in range(hp):
            if n_maps == 2:
                o0 = acc_ref[2 * hd, :, :LANES] / acc_ref[2 * hd, :, LANES:]
                o1 = acc_ref[2 * hd + 1, :, :LANES] / acc_ref[2 * hd + 1, :, LANES:]
                d = o0 - _da_lambda(lam_ref, lam_init) * o1
                y = _rms_rows(d, subg_ref[...]) * (1.0 - lam_init)
            else:
                y = acc_ref[hd, :, :LANES] / acc_ref[hd, :, LANES:]
            for g in range(groups):
                col = (hd * groups + g) * LANES
                o_ref[0, :, col:col + LANES] = y[g * tq:(g + 1) * tq].astype(o_ref.dtype)


def _flash(q, k, v, heads, hp, n_maps, groups, kd, extra=(), lam_init=0.0):
    b, s, _ = q.shape
    tq = min(TQ, s)
    rb = min(FLASH_ROW_BLOCK, tq)
    assert s % tq == 0 and tq % rb == 0 and heads % hp == 0
    nt = s // tq
    qi = jnp.asarray([i for i in range(nt) for _ in range(i + 1)], jnp.int32)
    kj = jnp.asarray([j for i in range(nt) for j in range(i + 1)], jnp.int32)
    wq = hp * n_maps * groups * kd
    wo = hp * groups * LANES
    rows = groups * tq
    in_specs = [pl.BlockSpec((1, tq, wq), lambda bi, h, p, qi_r, kj_r: (bi, qi_r[p], h)),
                pl.BlockSpec((1, tq, hp * kd), lambda bi, h, p, qi_r, kj_r: (bi, kj_r[p], h)),
                pl.BlockSpec((1, tq, hp * LANES), lambda bi, h, p, qi_r, kj_r: (bi, kj_r[p], h))]
    for e in extra:
        in_specs.append(pl.BlockSpec(e.shape, lambda bi, h, p, qi_r, kj_r: (0, 0)))
    grid_spec = pltpu.PrefetchScalarGridSpec(
        num_scalar_prefetch=2,
        grid=(b, heads // hp, int(qi.shape[0])),
        in_specs=in_specs,
        out_specs=pl.BlockSpec((1, tq, wo), lambda bi, h, p, qi_r, kj_r: (bi, qi_r[p], h)),
        scratch_shapes=[pltpu.VMEM((hp * n_maps, rows, kd), BF16), pltpu.VMEM((hp, tq, 2 * LANES), BF16),
                        pltpu.VMEM((hp * n_maps, rows, LANES), F32),
                        pltpu.VMEM((hp * n_maps, rows, 2 * LANES), F32)],
    )
    return pl.pallas_call(
        functools.partial(_flash_kernel, hp=hp, n_maps=n_maps, groups=groups, kd=kd, lam_init=lam_init, rb=rb),
        grid_spec=grid_spec,
        out_shape=jax.ShapeDtypeStruct((b, s, heads * groups * LANES), BF16),
        compiler_params=_params("parallel", "parallel", "arbitrary"),
        name="flash_da" if n_maps == 2 else "flash_mla",
    )(qi, kj, q, k, v, *extra)


def _decode_da_kernel(pt_ref, q_ref, ks_ref, vs_ref, lam_ref, subg_ref, *rest, pages, lam_init):
    k_refs, v_refs = rest[:pages], rest[pages:2 * pages]
    o_ref, qst_ref, kb_ref, vb_ref, m_ref, l_ref, acc_ref = rest[2 * pages:]
    step = pl.program_id(1)
    t_new = q_ref.shape[1]
    hrows = 4 * t_new

    def attend(scores, values, mask):
        s = jnp.concatenate([scores(h) for h in range(DA_KV_HEADS)], axis=0)
        if mask is not None:
            s = jnp.where(mask, s, MASKED)
        p, corr, m_new, l_new = _online_update(s, m_ref[...], l_ref[...])
        for h in range(DA_KV_HEADS):
            rs = slice(h * hrows, (h + 1) * hrows)
            acc_ref[rs] = acc_ref[rs] * corr[rs] + jnp.dot(p[rs].astype(BF16), values(h),
                                                          preferred_element_type=F32)
        m_ref[...] = m_new
        l_ref[...] = l_new

    @pl.when(step == 0)
    def _init():
        q = q_ref[0]
        for h in range(DA_KV_HEADS):
            parts = [q[:, ((h * 2 + mp) * 2 + g) * LANES:((h * 2 + mp) * 2 + g + 1) * LANES]
                     for mp in range(2) for g in range(2)]
            qst_ref[h] = jnp.concatenate(parts, axis=0).astype(BF16)
        m_ref[...] = jnp.full(m_ref.shape, MASKED, F32)
        l_ref[...] = jnp.zeros(l_ref.shape, F32)
        acc_ref[...] = jnp.zeros(acc_ref.shape, F32)
        pad = jnp.zeros((PAGE - t_new, DA_KV_HEADS * LANES), F32)
        k_own = jnp.concatenate([ks_ref[0], pad], axis=0).astype(BF16)
        v_own = jnp.concatenate([vs_ref[0], pad], axis=0).astype(BF16)
        r = lax.broadcasted_iota(jnp.int32, (DA_KV_HEADS * hrows, PAGE), 0)
        c = lax.broadcasted_iota(jnp.int32, (DA_KV_HEADS * hrows, PAGE), 1)
        attend(lambda h: lax.dot_general(qst_ref[h], k_own[:, h * LANES:(h + 1) * LANES], _NT,
                                         preferred_element_type=F32),
               lambda h: v_own[:, h * LANES:(h + 1) * LANES],
               c <= jnp.bitwise_and(r, t_new - 1))

    for r in range(pages):
        kb_ref[:, r * PAGE:(r + 1) * PAGE] = k_refs[r][0, 0].astype(BF16)
        for h in range(DA_KV_HEADS):
            vb_ref[h, r * PAGE:(r + 1) * PAGE, :] = (
                v_refs[r][0, 0, pl.ds(h, PAGE, stride=DA_KV_HEADS), :].astype(BF16))
    attend(lambda h: jnp.dot(qst_ref[h], kb_ref[h * LANES:(h + 1) * LANES, :], preferred_element_type=F32),
           lambda h: vb_ref[h], None)

    @pl.when(step == pl.num_programs(1) - 1)
    def _finish():
        o = acc_ref[...] / l_ref[...]
        lam = _da_lambda(lam_ref, lam_init)
        for h in range(DA_KV_HEADS):
            d = o[h * hrows:h * hrows + 2 * t_new] - lam * o[h * hrows + 2 * t_new:(h + 1) * hrows]
            y = _rms_rows(d, subg_ref[...]) * (1.0 - lam_init)
            for g in range(2):
                o_ref[0, :, (h * 2 + g) * LANES:(h * 2 + g + 1) * LANES] = y[g * t_new:(g + 1) * t_new]


def _decode_da(page_table, q, k_own, v_own, lam_p, subg, cache_kt, cache_v, layer, lam_init):
    nb, t_new, wq = q.shape
    n_pages = page_table.shape[1]
    pages = DA_PAGES_PER_STEP
    assert n_pages % pages == 0 and t_new == SUBLANES
    wkv = DA_KV_HEADS * LANES

    def page_spec(r):
        return pl.BlockSpec((1, 1, wkv, PAGE),
                            lambda b, j, pt: (layer, pt[b * n_pages + j * pages + r], 0, 0))

    per_seq = lambda w: pl.BlockSpec((1, t_new, w), lambda b, j, pt: (b, 0, 0))
    const = lambda a: pl.BlockSpec(a.shape, lambda b, j, pt: (0, 0))
    rows = DA_KV_HEADS * 4 * t_new
    grid_spec = pltpu.PrefetchScalarGridSpec(
        num_scalar_prefetch=1,
        grid=(nb, n_pages // pages),
        in_specs=([per_seq(wq), per_seq(wkv), per_seq(wkv), const(lam_p), const(subg)]
                  + [page_spec(r) for r in range(pages)] * 2),
        out_specs=per_seq(2 * wkv),
        scratch_shapes=[pltpu.VMEM((DA_KV_HEADS, 4 * t_new, LANES), BF16),
                        pltpu.VMEM((wkv, pages * PAGE), BF16),
                        pltpu.VMEM((DA_KV_HEADS, pages * PAGE, LANES), BF16),
                        pltpu.VMEM((rows, 1), F32), pltpu.VMEM((rows, 1), F32),
                        pltpu.VMEM((rows, LANES), F32)],
    )
    return pl.pallas_call(
        functools.partial(_decode_da_kernel, pages=pages, lam_init=lam_init),
        grid_spec=grid_spec,
        out_shape=jax.ShapeDtypeStruct((nb, t_new, 2 * wkv), F32),
        compiler_params=_params("parallel", "arbitrary"),
        name="decode_da",
    )(page_table.reshape(-1), q, k_own, v_own, lam_p, subg,
      *([cache_kt] * pages), *([cache_v] * pages))


def _decode_mla_kernel(pt_ref, q_ref, cs_ref, rs_ref, wukt_ref, wuv_ref, *rest, pages, chunk):
    c_refs, r_refs = rest[:pages], rest[pages:2 * pages]
    o_ref, lhs_ref, qr_ref, cc_ref, rb_ref, m_ref, l_ref, acc_ref = rest[2 * pages:]
    step = pl.program_id(1)
    t_new = q_ref.shape[1]
    nq = MLA_HEADS * t_new
    ne = MLA_HEADS * MLA_NOPE

    def scores(tok, bp):
        res = lax.dot_general(lhs_ref[...], tok, _NT, preferred_element_type=F32)
        e2 = res[:ne] * res[:ne]
        a = res[ne:]
        s_parts = []
        for h in range(MLA_HEADS):
            ms = jnp.sum(e2[h * MLA_NOPE:(h + 1) * MLA_NOPE], axis=0, keepdims=True) * (1.0 / MLA_NOPE)
            s_parts.append(a[h * t_new:(h + 1) * t_new] * lax.rsqrt(ms + EPS)
                           + bp[h * t_new:(h + 1) * t_new])
        return jnp.concatenate(s_parts, axis=0)

    def fold(s, tok):
        cols = [s[:, c * LANES:(c + 1) * LANES] for c in range(s.shape[1] // LANES)]
        mx = cols[0]
        for col in cols[1:]:
            mx = jnp.maximum(mx, col)
        m_prev = m_ref[...]
        m_new = jnp.maximum(m_prev, jnp.max(mx, axis=-1, keepdims=True))
        corr = jnp.exp(m_prev - m_new)
        p_cols = [jnp.exp(col - m_new) for col in cols]
        tot = p_cols[0]
        for pc in p_cols[1:]:
            tot = tot + pc
        l_ref[...] = l_ref[...] * corr + jnp.sum(tot, axis=-1, keepdims=True)
        p = jnp.concatenate([pc.astype(BF16) for pc in p_cols], axis=1)
        acc_ref[...] = acc_ref[...] * corr + jnp.dot(p, tok, preferred_element_type=F32)
        m_ref[...] = m_new

    @pl.when(step == 0)
    def _init():
        q = q_ref[0]
        qa = jnp.concatenate([q[:, h * 2 * LANES:h * 2 * LANES + LANES] for h in range(MLA_HEADS)], axis=0)
        qr = jnp.concatenate([q[:, h * 2 * LANES + LANES:(h + 1) * 2 * LANES] for h in range(MLA_HEADS)], axis=0)
        lhs_ref[0:ne, :] = wukt_ref[...]
        lhs_ref[ne:ne + nq, :] = qa.astype(BF16)
        qr_ref[...] = qr.astype(BF16)
        rb_ref[MLA_ROPE:, :] = jnp.zeros((LANES - MLA_ROPE, rb_ref.shape[1]), BF16)
        m_ref[...] = jnp.full(m_ref.shape, MASKED, F32)
        l_ref[...] = jnp.zeros(l_ref.shape, F32)
        acc_ref[...] = jnp.zeros(acc_ref.shape, F32)
        pad = jnp.zeros((PAGE - t_new, LANES), F32)
        c_own = jnp.concatenate([cs_ref[0], pad], axis=0).astype(BF16)
        r_own = jnp.concatenate([rs_ref[0], pad], axis=0).astype(BF16)
        r = lax.broadcasted_iota(jnp.int32, (nq, PAGE), 0)
        c = lax.broadcasted_iota(jnp.int32, (nq, PAGE), 1)
        s_own = scores(c_own, lax.dot_general(qr_ref[...], r_own, _NT, preferred_element_type=F32))
        fold(jnp.where(c <= jnp.bitwise_and(r, t_new - 1), s_own, MASKED), c_own)

    for r in range(pages):
        cc_ref[r * PAGE:(r + 1) * PAGE, :] = c_refs[r][0, 0].astype(BF16)
        rb_ref[0:MLA_ROPE, r * PAGE:(r + 1) * PAGE] = r_refs[r][0, 0].astype(BF16)
    s_chunks = []
    for ck in range(pages * PAGE // chunk):
        cs = slice(ck * chunk, (ck + 1) * chunk)
        s_chunks.append(scores(cc_ref[cs, :], jnp.dot(qr_ref[...], rb_ref[:, cs], preferred_element_type=F32)))
    fold(jnp.concatenate(s_chunks, axis=1), cc_ref[...])

    @pl.when(step == pl.num_programs(1) - 1)
    def _finish():
        o_lat = (acc_ref[...] / l_ref[...]).astype(BF16)
        o_all = jnp.dot(o_lat, wuv_ref[...], preferred_element_type=F32)
        for h in range(MLA_HEADS):
            o_ref[0, :, h * MLA_V:(h + 1) * MLA_V] = o_all[h * t_new:(h + 1) * t_new,
                                                           h * MLA_V:(h + 1) * MLA_V]


def _decode_mla(page_table, q, c_own, r_own, wukt, wuv, cache_c, cache_rt, layer):
    nb, t_new, wq = q.shape
    n_pages = page_table.shape[1]
    pages = MLA_PAGES_PER_STEP
    chunk = MLA_CHUNK
    assert n_pages % pages == 0 and (pages * PAGE) % chunk == 0 and t_new == SUBLANES

    def page_spec(r, rows):
        return pl.BlockSpec((1, 1, rows, PAGE),
                            lambda b, j, pt: (layer, pt[b * n_pages + j * pages + r], 0, 0))

    per_seq = lambda w: pl.BlockSpec((1, t_new, w), lambda b, j, pt: (b, 0, 0))
    const = lambda a: pl.BlockSpec(a.shape, lambda b, j, pt: (0, 0))
    nq = MLA_HEADS * t_new
    grid_spec = pltpu.PrefetchScalarGridSpec(
        num_scalar_prefetch=1,
        grid=(nb, n_pages // pages),
        in_specs=([per_seq(wq), per_seq(MLA_KV_RANK), per_seq(LANES), const(wukt), const(wuv)]
                  + [page_spec(r, PAGE) for r in range(pages)]
                  + [page_spec(r, MLA_ROPE) for r in range(pages)]),
        out_specs=per_seq(MLA_HEADS * MLA_V),
        scratch_shapes=[pltpu.VMEM((MLA_HEADS * MLA_NOPE + nq, MLA_KV_RANK), BF16),
                        pltpu.VMEM((nq, LANES), BF16),
                        pltpu.VMEM((pages * PAGE, MLA_KV_RANK), BF16),
                        pltpu.VMEM((LANES, pages * PAGE), BF16),
                        pltpu.VMEM((nq, LANES), F32), pltpu.VMEM((nq, LANES), F32),
                        pltpu.VMEM((nq, MLA_KV_RANK), F32)],
    )
    return pl.pallas_call(
        functools.partial(_decode_mla_kernel, pages=pages, chunk=chunk),
        grid_spec=grid_spec,
        out_shape=jax.ShapeDtypeStruct((nb, t_new, MLA_HEADS * MLA_V), F32),
        compiler_params=_params("parallel", "arbitrary"),
        name="decode_mla",
    )(page_table.reshape(-1), q, c_own, r_own, wukt, wuv,
      *([cache_c] * pages), *([cache_rt] * pages))


def _ffn_kernel(x_ref, a_ref, wo_ref, gn_ref, wgu_ref, cw_ref, cb_ref, wd_ref, prev_ref,
                xo_ref, go_ref, *carry, tiles_per_seq):
    i = pl.program_id(0)
    tm = x_ref.shape[0]
    nf, tf = wd_ref.shape[0], wd_ref.shape[1]

    x1 = x_ref[...] + jnp.dot(a_ref[...].astype(BF16), wo_ref[...], preferred_element_type=F32)
    h = _rms_rows(x1, gn_ref[...]).astype(BF16)
    row = lax.broadcasted_iota(jnp.int32, (tm, tf), 0)
    t = row if tiles_per_seq else jnp.bitwise_and(row, SUBLANES - 1)
    cw, cb = cw_ref[...], cb_ref[...]
    if tiles_per_seq:
        (carry_ref,) = carry

        @pl.when(i % tiles_per_seq == 0)
        def _seq_start():
            for c in range(nf):
                carry_ref[c, SUBLANES - 2:SUBLANES, :] = prev_ref[0, :, c * tf:(c + 1) * tf]

    out = x1
    for c in range(nf):
        cs = slice(c * tf, (c + 1) * tf)
        g = jnp.dot(h, wgu_ref[c], preferred_element_type=F32)
        u = jnp.dot(h, wgu_ref[nf + c], preferred_element_type=F32)
        if tiles_per_seq:
            tail = carry_ref[c]
            p0, p1 = tail[SUBLANES - 2:SUBLANES - 1], tail[SUBLANES - 1:SUBLANES]
            carry_ref[c] = g[tm - SUBLANES:]
            go_ref[0, :, cs] = g[tm - SUBLANES:]
        else:
            nseq = tm // SUBLANES
            prev = prev_ref[:, :, cs]
            spread = lambda v: jnp.broadcast_to(v, (nseq, SUBLANES, tf)).reshape(tm, tf)
            p0, p1 = spread(prev[:, 0:1, :]), spread(prev[:, 1:2, :])
            go_ref[:, cs] = g
        g1 = jnp.where(t == 0, p1, pltpu.roll(g, 1, axis=0))
        g2 = jnp.where(t == 0, p0, jnp.where(t == 1, p1, pltpu.roll(g, 2, axis=0)))
        gc = cb[:, cs] + cw[0:1, cs] * g2 + cw[1:2, cs] * g1 + cw[2:3, cs] * g
        act = gc * (1.0 / (1.0 + jnp.exp(-gc))) * u
        out = out + jnp.dot(act.astype(BF16), wd_ref[c], preferred_element_type=F32)
    xo_ref[...] = out


def _ffn(x, attn, w_out, gn, w_gu, conv_w, conv_b, w_down, prev, layer, tm, tiles_per_seq):
    n = x.shape[0]
    nf, tf = w_down.shape[1], w_down.shape[2]
    nt = n // tm
    row = lambda i: (i, 0)
    const2 = lambda i: (0, 0)
    of_layer = lambda i: (layer, 0, 0, 0)
    once = dict(pipeline_mode=pl.Buffered(1))
    if tiles_per_seq:
        prev_spec = pl.BlockSpec((1, 2, D_FF), lambda i: (i // tiles_per_seq, 0, 0))
        go_spec = pl.BlockSpec((1, SUBLANES, D_FF), lambda i: (i, 0, 0))
        go_shape = jax.ShapeDtypeStruct((nt, SUBLANES, D_FF), F32)
        scratch = [pltpu.VMEM((nf, SUBLANES, tf), F32)]
    else:
        prev_spec = pl.BlockSpec((tm // SUBLANES, 2, D_FF), lambda i: (i, 0, 0))
        go_spec = pl.BlockSpec((tm, D_FF), row)
        go_shape = jax.ShapeDtypeStruct((n, D_FF), F32)
        scratch = []
    return pl.pallas_call(
        functools.partial(_ffn_kernel, tiles_per_seq=tiles_per_seq),
        grid=(nt,),
        in_specs=[pl.BlockSpec((tm, D_MODEL), row), pl.BlockSpec((tm, D_MODEL), row),
                  pl.BlockSpec((D_MODEL, D_MODEL), const2, **once), pl.BlockSpec((1, D_MODEL), const2),
                  pl.BlockSpec((None,) + w_gu.shape[1:], of_layer, **once),
                  pl.BlockSpec((3, D_FF), const2), pl.BlockSpec((1, D_FF), const2),
                  pl.BlockSpec((None,) + w_down.shape[1:], of_layer, **once), prev_spec],
        out_specs=[pl.BlockSpec((tm, D_MODEL), row), go_spec],
        out_shape=[jax.ShapeDtypeStruct((n, D_MODEL), F32), go_shape],
        scratch_shapes=scratch,
        compiler_params=_params("arbitrary"),
        name="conv_ffn_seq" if tiles_per_seq else "conv_ffn_blocks",
    )(x, attn, w_out, gn, w_gu, conv_w, conv_b, w_down, prev)


def _rope_tables(pos, rot, theta, rows):
    half = rot // 2
    inv = 1.0 / (theta ** (jnp.arange(half, dtype=F32) / half))
    ang = pos.astype(F32)[:, None] * inv[None, :]
    d = jnp.arange(LANES) % 64
    idx = d % half
    cos = jnp.where(d < rot, jnp.cos(ang)[:, idx], 1.0)
    sin = jnp.sin(ang)[:, idx]
    sin_up = jnp.where(d < half, -sin, 0.0)
    sin_dn = jnp.where((d >= half) & (d < rot), sin, 0.0)
    reps = rows // pos.shape[0] if pos.shape[0] < rows else 1
    return tuple(jnp.tile(t.astype(F32), (reps, 1)) for t in (cos, sin_up, sin_dn))


def _row(v):
    return v.reshape(1, -1).astype(F32)


def _tile2(v):
    return jnp.tile(v.astype(F32), 2).reshape(1, LANES)


def kernel(x_prompt, x_sample, cache_da_k, cache_da_v, cache_mla_ckv, cache_mla_kpe, state_conv, page_table,
           attn_norm, ffn_norm, da_w_in, da_qk_norm, da_lambda, da_sub_norm, da_w_out,
           mla_w_in, mla_q_a_norm, mla_kv_a_norm, mla_w_uq, mla_w_ukv, mla_qk_norm, mla_w_out,
           ffn_w_gu, ffn_conv_w, ffn_conv_b, ffn_w_down):
    b, s, d = x_prompt.shape
    nb, t_new, _ = x_sample.shape
    n_pages = page_table.shape[1]
    depth = ffn_w_gu.shape[0]
    n_pool = cache_da_k.shape[1]
    assert d == D_MODEL and s % TM_FFN_PROMPT == 0 and (nb * t_new) % TM_PROJ == 0

    pos_p = jnp.arange(s)
    pos_s = n_pages * PAGE + jnp.arange(t_new)
    da_tab_p = _rope_tables(pos_p, 2 * DA_ROT_HALF, DA_THETA, TM_PROJ)
    da_tab_s = _rope_tables(pos_s, 2 * DA_ROT_HALF, DA_THETA, TM_PROJ)
    mla_tab_p = _rope_tables(pos_p, MLA_ROPE, MLA_THETA, TM_PROJ)
    mla_tab_s = _rope_tables(pos_s, MLA_ROPE, MLA_THETA, TM_PROJ)

    ck = jnp.transpose(cache_da_k, (0, 1, 3, 4, 5, 2)).reshape(-1, n_pool, DA_KV_HEADS * LANES, PAGE)
    cv = cache_da_v.reshape(-1, n_pool, PAGE * DA_KV_HEADS, LANES)
    cr = jnp.transpose(cache_mla_kpe, (0, 1, 3, 2))
    zero_prev = jnp.zeros((b, 2, D_FF), F32)
    nf = D_FF // TF
    w_gu = jnp.transpose(ffn_w_gu.astype(BF16).reshape(depth, D_MODEL, 2 * nf, TF), (0, 2, 1, 3))
    w_down = ffn_w_down.astype(BF16).reshape(depth, nf, TF, D_MODEL)

    xp = x_prompt.reshape(b * s, d)
    xs = x_sample.reshape(nb * t_new, d)
    out = {name: [] for name in ("dkp", "dvp", "ckp", "kep", "cvp", "dks", "dvs", "cks", "kes", "cvs")}
    for i in range(depth):
        li = i // N_MIXERS
        gn = _row(attn_norm[i])
        if i % N_MIXERS == 0:
            lam_init = 0.8 - 0.6 * math.exp(-0.3 * i)
            w_in = da_w_in[li].astype(BF16)
            gq, gk = _tile2(da_qk_norm[li, 0]), _tile2(da_qk_norm[li, 1])
            lam_p, subg = da_lambda[li].astype(F32), _row(da_sub_norm[li])
            w_out = da_w_out[li].astype(BF16)
            qp, kp, vp, kpb, vpb = _proj_da(xp, gn, w_in, gq, gk, da_tab_p, BF16)
            ap = _flash(qp.reshape(b, s, -1), kpb.reshape(b, s, -1), vpb.reshape(b, s, -1),
                        DA_KV_HEADS, 1, 2, 2, LANES, extra=(lam_p, subg), lam_init=lam_init)
            qs, ks, vs, _, _ = _proj_da(xs, gn, w_in, gq, gk, da_tab_s, F32)
            a_s = _decode_da(page_table, qs.reshape(nb, t_new, -1), ks.reshape(nb, t_new, -1),
                             vs.reshape(nb, t_new, -1), lam_p, subg, ck, cv, li, lam_init)
            out["dkp"].append(kp.reshape(b, s, DA_KV_HEADS, 2, DA_HEAD_DIM))
            out["dvp"].append(vp.reshape(b, s, DA_KV_HEADS, 2 * DA_HEAD_DIM))
            out["dks"].append(ks.reshape(nb, t_new, DA_KV_HEADS, 2, DA_HEAD_DIM))
            out["dvs"].append(vs.reshape(nb, t_new, DA_KV_HEADS, 2 * DA_HEAD_DIM))
        else:
            w_in = jnp.pad(mla_w_in[li], ((0, 0), (0, 4 * LANES - mla_w_in.shape[2]))).astype(BF16)
            w_uq = mla_w_uq[li].reshape(MLA_Q_RANK, MLA_HEADS, MLA_NOPE + MLA_ROPE)
            w_uq = jnp.concatenate([w_uq[:, :, :MLA_NOPE].reshape(MLA_Q_RANK, -1),
                                    w_uq[:, :, MLA_NOPE:].reshape(MLA_Q_RANK, -1)], axis=1).astype(BF16)
            w_uk = mla_w_ukv[li, :, :, :MLA_NOPE]
            w_uv = mla_w_ukv[li, :, :, MLA_NOPE:].reshape(MLA_KV_RANK, -1)
            w_ukv = jnp.concatenate([w_uk.reshape(MLA_KV_RANK, -1), w_uv], axis=1).astype(BF16)
            w_ukt = jnp.transpose(w_uk, (1, 2, 0)).astype(BF16)
            qag, kvag = _row(mla_q_a_norm[li]), _row(mla_kv_a_norm[li])
            gqn, kg = _row(mla_qk_norm[li, 0, :MLA_NOPE]), _row(mla_qk_norm[li, 1, :MLA_NOPE])
            gqr, gkr = _tile2(mla_qk_norm[li, 0, MLA_NOPE:]), _tile2(mla_qk_norm[li, 1, MLA_NOPE:])
            w_out = mla_w_out[li].astype(BF16)
            qp, ckvp, krp, kcat, vpb = _proj_mla(xp, gn, w_in, qag, kvag, w_uq, gqn, gqr, gkr, kg,
                                                 mla_tab_p, w_ukv, absorb=False)
            ap = _flash(qp.reshape(b, s, -1), kcat.reshape(b, s, -1), vpb.reshape(b, s, -1),
                        MLA_HEADS, FLASH_MLA_HEADS_PER_STEP, 1, 1, 2 * LANES)
            qs, ckvs, krs, krs_pad = _proj_mla(xs, gn, w_in, qag, kvag, w_uq, gqn, gqr, gkr, kg,
                                               mla_tab_s, w_ukt, absorb=True)
            a_s = _decode_mla(page_table, qs.reshape(nb, t_new, -1), ckvs.reshape(nb, t_new, -1),
                              krs_pad.reshape(nb, t_new, -1), w_ukt.reshape(-1, MLA_KV_RANK),
                              w_uv.astype(BF16), cache_mla_ckv, cr, li)
            out["ckp"].append(ckvp.reshape(b, s, MLA_KV_RANK))
            out["kep"].append(krp.reshape(b, s, MLA_ROPE))
            out["cks"].append(ckvs.reshape(nb, t_new, MLA_KV_RANK))
            out["kes"].append(krs.reshape(nb, t_new, MLA_ROPE))
        fgn = _row(ffn_norm[i])
        cw, cb = ffn_conv_w[i].astype(F32), _row(ffn_conv_b[i])
        tps = s // TM_FFN_PROMPT
        xp, gp = _ffn(xp, ap.reshape(b * s, d), w_out, fgn, w_gu, cw, cb, w_down, zero_prev, i,
                      TM_FFN_PROMPT, tps)
        xs, gs = _ffn(xs, a_s.reshape(nb * t_new, d), w_out, fgn, w_gu, cw, cb, w_down, state_conv[i], i,
                      TM_FFN_SAMPLE, 0)
        out["cvp"].append(gp.reshape(b, tps, SUBLANES, D_FF)[:, -1, SUBLANES - 2:, :])
        out["cvs"].append(gs.reshape(nb, t_new, D_FF)[:, t_new - 2:, :])
    st = jnp.stack
    return (xp.reshape(b, s, d), xs.reshape(nb, t_new, d), st(out["dkp"]), st(out["dvp"]), st(out["ckp"]),
            st(out["kep"]), st(out["cvp"]), st(out["dks"]), st(out["dvs"]), st(out["cks"]), st(out["kes"]),
            st(out["cvs"]))
```

```python
import functools
import math

import jax
import jax.numpy as jnp
from jax import lax
from jax.experimental import pallas as pl
from jax.experimental.pallas import tpu as pltpu

F32 = jnp.float32
BF16 = jnp.bfloat16

EPS = 1e-6
LANES = 128
SUBLANES = 8
PAGE = 128
MASKED = -1e30
VMEM_LIMIT = 56 * 1024 * 1024

D_MODEL = 1024
DA_KV_HEADS = 4
DA_HEAD_DIM = 64
DA_ROT_HALF = 8
DA_THETA = 500000.0
MLA_HEADS = 8
MLA_Q_RANK = 256
MLA_KV_RANK = 128
MLA_NOPE = 128
MLA_ROPE = 64
MLA_V = 128
MLA_THETA = 10000.0
D_FF = 2816
N_MIXERS = 2

TM_PROJ = 512
TQ = 1024
FLASH_MLA_HEADS_PER_STEP = 2
FLASH_ROW_BLOCK = 256
TM_FFN_PROMPT = 1024
TM_FFN_SAMPLE = 256
TF = 256
DA_PAGES_PER_STEP = 32
MLA_PAGES_PER_STEP = 32
MLA_CHUNK = 256

_NT = (((1,), (1,)), ((), ()))


def _params(*semantics):
    return pltpu.CompilerParams(dimension_semantics=semantics, vmem_limit_bytes=VMEM_LIMIT)


def _rms_rows(x, g):
    return x * lax.rsqrt(jnp.mean(x * x, axis=-1, keepdims=True) + EPS) * g


def _group_sumsq(x, group):
    x2 = x * x
    if group == LANES:
        return jnp.sum(x2, axis=-1, keepdims=True)
    shift = int(math.log2(group))
    r = lax.shift_right_logical(lax.broadcasted_iota(jnp.int32, (LANES, LANES), 0), shift)
    c = lax.shift_right_logical(lax.broadcasted_iota(jnp.int32, (LANES, LANES), 1), shift)
    ones = jnp.where(r == c, 1.0, 0.0).astype(BF16)
    hi = x2.astype(BF16)
    lo = (x2 - hi.astype(F32)).astype(BF16)
    return (jnp.dot(hi, ones, preferred_element_type=F32)
            + jnp.dot(lo, ones, preferred_element_type=F32))


def _group_rms(x, g, group):
    return x * lax.rsqrt(_group_sumsq(x, group) * (1.0 / group) + EPS) * g


def _rope(x, cos, sin_up, sin_dn, half):
    x_up = pltpu.roll(x, LANES - half, axis=1)
    x_dn = pltpu.roll(x, half, axis=1)
    return x * cos + x_up * sin_up + x_dn * sin_dn


def _online_update(s, m_prev, l_prev):
    m_new = jnp.maximum(m_prev, jnp.max(s, axis=-1, keepdims=True))
    corr = jnp.exp(m_prev - m_new)
    p = jnp.exp(s - m_new)
    l_new = l_prev * corr + jnp.sum(p, axis=-1, keepdims=True)
    return p, corr, m_new, l_new


def _proj_da_kernel(x_ref, gn_ref, w_ref, gq_ref, gk_ref, cos_ref, sup_ref, sdn_ref,
                    q_ref, k_ref, v_ref, kb_ref, vb_ref):
    tm = x_ref.shape[0]
    h = _rms_rows(x_ref[...], gn_ref[...]).astype(BF16)
    y = jnp.dot(h, w_ref[...], preferred_element_type=F32)
    cos, sup, sdn = cos_ref[...], sup_ref[...], sdn_ref[...]
    low = lax.broadcasted_iota(jnp.int32, (tm, LANES), 1) < DA_HEAD_DIM
    scale = DA_HEAD_DIM ** -0.5

    def norm_rope(blk, g):
        return _rope(_group_rms(blk, g, DA_HEAD_DIM), cos, sup, sdn, DA_ROT_HALF)

    gq, gk = gq_ref[...], gk_ref[...]
    for c in range(2 * DA_KV_HEADS):
        qb = norm_rope(y[:, c * LANES:(c + 1) * LANES], gq) * scale
        kvh, grp = divmod(c, 2)
        for mp in range(2):
            oc = (kvh * 2 + mp) * 2 + grp
            qm = jnp.where(low if mp == 0 else jnp.logical_not(low), qb, 0.0)
            q_ref[:, oc * LANES:(oc + 1) * LANES] = qm.astype(q_ref.dtype)
    k_off = 2 * DA_KV_HEADS * LANES
    for c in range(DA_KV_HEADS):
        kb = norm_rope(y[:, k_off + c * LANES:k_off + (c + 1) * LANES], gk)
        k_ref[:, c * LANES:(c + 1) * LANES] = kb
        kb_ref[:, c * LANES:(c + 1) * LANES] = kb.astype(BF16)
    v = y[:, k_off + DA_KV_HEADS * LANES:]
    v_ref[...] = v
    vb_ref[...] = v.astype(BF16)


def _proj_da(x, gn, w_in, gq, gk, tables, q_dtype):
    n = x.shape[0]
    tm = TM_PROJ
    cos, sup, sdn = tables
    t_blocks = cos.shape[0] // tm
    row = lambda i: (i, 0)
    const = lambda i: (0, 0)
    tab = lambda i: (i % t_blocks, 0)
    nq, nk = 4 * DA_KV_HEADS * 2 * LANES, DA_KV_HEADS * LANES
    return pl.pallas_call(
        _proj_da_kernel,
        grid=(n // tm,),
        in_specs=[pl.BlockSpec((tm, D_MODEL), row), pl.BlockSpec((1, D_MODEL), const),
                  pl.BlockSpec(w_in.shape, const), pl.BlockSpec((1, LANES), const),
                  pl.BlockSpec((1, LANES), const), pl.BlockSpec((tm, LANES), tab),
                  pl.BlockSpec((tm, LANES), tab), pl.BlockSpec((tm, LANES), tab)],
        out_specs=[pl.BlockSpec((tm, nq), row), pl.BlockSpec((tm, nk), row), pl.BlockSpec((tm, nk), row),
                   pl.BlockSpec((tm, nk), row), pl.BlockSpec((tm, nk), row)],
        out_shape=[jax.ShapeDtypeStruct((n, nq), q_dtype), jax.ShapeDtypeStruct((n, nk), F32),
                   jax.ShapeDtypeStruct((n, nk), F32), jax.ShapeDtypeStruct((n, nk), BF16),
                   jax.ShapeDtypeStruct((n, nk), BF16)],
        compiler_params=_params("parallel"),
        name="proj_da",
    )(x, gn, w_in, gq, gk, cos, sup, sdn)


def _proj_mla_kernel(x_ref, gn_ref, win_ref, qag_ref, kvag_ref, wuq_ref, gqn_ref, gqr_ref, gkr_ref,
                     kg_ref, cos_ref, sup_ref, sdn_ref, wk_ref, q_ref, ckv_ref, kr_ref, *rest, absorb):
    tm = x_ref.shape[0]
    h = _rms_rows(x_ref[...], gn_ref[...]).astype(BF16)
    y = jnp.dot(h, win_ref[...], preferred_element_type=F32)
    cos, sup, sdn = cos_ref[...], sup_ref[...], sdn_ref[...]
    low = lax.broadcasted_iota(jnp.int32, (tm, LANES), 1) < MLA_ROPE
    scale = (MLA_NOPE + MLA_ROPE) ** -0.5
    half = MLA_ROPE // 2

    cqn = _rms_rows(y[:, :MLA_Q_RANK], qag_ref[...]).astype(BF16)
    q = jnp.dot(cqn, wuq_ref[...], preferred_element_type=F32)
    ckv = _rms_rows(y[:, MLA_Q_RANK:MLA_Q_RANK + MLA_KV_RANK], kvag_ref[...])
    ckv_ref[...] = ckv
    krp = _rope(_group_rms(y[:, MLA_Q_RANK + MLA_KV_RANK:], gkr_ref[...], MLA_ROPE), cos, sup, sdn, half)
    kr_ref[...] = krp[:, :MLA_ROPE]

    gqn, gqr, kg = gqn_ref[...], gqr_ref[...], kg_ref[...]
    if absorb:
        (krp_ref,) = rest
        krp_ref[...] = krp
    else:
        kcat_ref, v_ref = rest
        kv = jnp.dot(ckv.astype(BF16), wk_ref[...], preferred_element_type=F32)
        v_ref[...] = kv[:, MLA_HEADS * MLA_NOPE:].astype(BF16)
        krb = krp.astype(BF16)

    rope_off = MLA_HEADS * MLA_NOPE
    for hd in range(MLA_HEADS):
        qn = _group_rms(q[:, hd * LANES:(hd + 1) * LANES], gqn, LANES) * scale
        if absorb:
            qn = jnp.dot((qn * kg).astype(BF16), wk_ref[hd], preferred_element_type=F32)
        q_ref[:, hd * 2 * LANES:hd * 2 * LANES + LANES] = qn.astype(q_ref.dtype)
        if not absorb:
            kn = _group_rms(kv[:, hd * LANES:(hd + 1) * LANES], kg, LANES)
            kcat_ref[:, hd * 2 * LANES:hd * 2 * LANES + LANES] = kn.astype(BF16)
            kcat_ref[:, hd * 2 * LANES + LANES:(hd + 1) * 2 * LANES] = krb
    for c in range(MLA_HEADS // 2):
        qr = _rope(_group_rms(q[:, rope_off + c * LANES:rope_off + (c + 1) * LANES], gqr, MLA_ROPE),
                   cos, sup, sdn, half) * scale
        for par in range(2):
            hd = 2 * c + par
            src = qr if par == 0 else pltpu.roll(qr, MLA_ROPE, axis=1)
            q_ref[:, hd * 2 * LANES + LANES:(hd + 1) * 2 * LANES] = jnp.where(low, src, 0.0).astype(q_ref.dtype)


def _proj_mla(x, gn, w_in, qag, kvag, w_uq, gqn, gqr, gkr, kg, tables, wk, absorb):
    n = x.shape[0]
    tm = TM_PROJ
    cos, sup, sdn = tables
    t_blocks = cos.shape[0] // tm
    row = lambda i: (i, 0)
    const = lambda i: (0, 0)
    tab = lambda i: (i % t_blocks, 0)
    wk_spec = (pl.BlockSpec(wk.shape, lambda i: (0, 0, 0)) if absorb else pl.BlockSpec(wk.shape, const))
    nq = MLA_HEADS * 2 * LANES
    out_specs = [pl.BlockSpec((tm, nq), row), pl.BlockSpec((tm, MLA_KV_RANK), row),
                 pl.BlockSpec((tm, MLA_ROPE), row)]
    out_shape = [jax.ShapeDtypeStruct((n, nq), F32 if absorb else BF16),
                 jax.ShapeDtypeStruct((n, MLA_KV_RANK), F32), jax.ShapeDtypeStruct((n, MLA_ROPE), F32)]
    if absorb:
        out_specs.append(pl.BlockSpec((tm, LANES), row))
        out_shape.append(jax.ShapeDtypeStruct((n, LANES), F32))
    else:
        out_specs += [pl.BlockSpec((tm, nq), row), pl.BlockSpec((tm, MLA_HEADS * MLA_V), row)]
        out_shape += [jax.ShapeDtypeStruct((n, nq), BF16), jax.ShapeDtypeStruct((n, MLA_HEADS * MLA_V), BF16)]
    vec = lambda w: pl.BlockSpec((1, w), const)
    return pl.pallas_call(
        functools.partial(_proj_mla_kernel, absorb=absorb),
        grid=(n // tm,),
        in_specs=[pl.BlockSpec((tm, D_MODEL), row), vec(D_MODEL), pl.BlockSpec(w_in.shape, const),
                  vec(MLA_Q_RANK), vec(MLA_KV_RANK), pl.BlockSpec(w_uq.shape, const),
                  vec(LANES), vec(LANES), vec(LANES), vec(LANES),
                  pl.BlockSpec((tm, LANES), tab), pl.BlockSpec((tm, LANES), tab),
                  pl.BlockSpec((tm, LANES), tab), wk_spec],
        out_specs=out_specs,
        out_shape=out_shape,
        compiler_params=_params("parallel"),
        name="proj_mla_absorb" if absorb else "proj_mla",
    )(x, gn, w_in, qag, kvag, w_uq, gqn, gqr, gkr, kg, cos, sup, sdn, wk)


def _da_lambda(lam_ref, lam_init):
    lp = lam_ref[...]
    a = jnp.sum(lp[0:1] * lp[1:2], axis=-1, keepdims=True)
    b = jnp.sum(lp[2:3] * lp[3:4], axis=-1, keepdims=True)
    return jnp.exp(a) - jnp.exp(b) + lam_init


def _flash_kernel(qi_ref, kj_ref, q_ref, k_ref, v_ref, *rest, hp, n_maps, groups, kd, lam_init, rb):
    if n_maps == 2:
        lam_ref, subg_ref, o_ref, qs_ref, va_ref, m_ref, acc_ref = rest
    else:
        o_ref, qs_ref, va_ref, m_ref, acc_ref = rest
    tq = q_ref.shape[1]
    tk = k_ref.shape[1]
    rows = groups * tq
    pair = pl.program_id(2)
    qi = qi_ref[pair]
    kj = kj_ref[pair]

    @pl.when(kj == 0)
    def _init():
        for hd in range(hp):
            for mp in range(n_maps):
                for g in range(groups):
                    col = ((hd * n_maps + mp) * groups + g) * kd
                    qs_ref[hd * n_maps + mp, g * tq:(g + 1) * tq, :] = q_ref[0, :, col:col + kd]
            va_ref[hd, :, LANES:] = jnp.ones((tk, LANES), BF16)
        m_ref[...] = jnp.full(m_ref.shape, MASKED, F32)
        acc_ref[...] = jnp.zeros(acc_ref.shape, F32)

    def step(diagonal):
        for hd in range(hp):
            va_ref[hd, :, :LANES] = v_ref[0, :, hd * LANES:(hd + 1) * LANES]
        for blk in range(rows // rb):
            lo = (blk * rb) % tq
            keys = lo + rb if diagonal else tk
            rs = slice(blk * rb, (blk + 1) * rb)
            for hm in range(hp * n_maps):
                hd = hm // n_maps
                s = lax.dot_general(qs_ref[hm, rs, :], k_ref[0, :keys, hd * kd:(hd + 1) * kd], _NT,
                                    preferred_element_type=F32)
                cols = [s[:, c * LANES:(c + 1) * LANES] for c in range(keys // LANES)]
                if diagonal:
                    r = lax.broadcasted_iota(jnp.int32, (rb, LANES), 0) + lo
                    c = lax.broadcasted_iota(jnp.int32, (rb, LANES), 1)
                    for cb in range(lo // LANES, keys // LANES):
                        cols[cb] = jnp.where(c + cb * LANES <= r, cols[cb], MASKED)
                mx = cols[0]
                for col in cols[1:]:
                    mx = jnp.maximum(mx, col)
                m_prev = m_ref[hm, rs, :]
                m_new = jnp.maximum(m_prev, jnp.max(mx, axis=-1, keepdims=True))
                corr = jnp.exp(m_prev - m_new)
                p = jnp.concatenate([jnp.exp(col - m_new).astype(BF16) for col in cols], axis=1)
                pv = jnp.dot(p, va_ref[hd, :keys, :], preferred_element_type=F32)
                acc_ref[hm, rs, :] = acc_ref[hm, rs, :] * jnp.concatenate([corr, corr], axis=1) + pv
                m_ref[hm, rs, :] = m_new

    @pl.when(kj < qi)
    def _full():
        step(False)

    @pl.when(kj == qi)
    def _diag():
        step(True)
        for hd in range(hp):
            if n_maps == 2:
                o0 = acc_ref[2 * hd, :, :LANES] / acc_ref[2 * hd, :, LANES:]
                o1 = acc_ref[2 * hd + 1, :, :LANES] / acc_ref[2 * hd + 1, :, LANES:]
                d = o0 - _da_lambda(lam_ref, lam_init) * o1
                y = _rms_rows(d, subg_ref[...]) * (1.0 - lam_init)
            else:
                y = acc_ref[hd, :, :LANES] / acc_ref[hd, :, LANES:]
            for g in range(groups):
                col = (hd * groups + g) * LANES
                o_ref[0, :, col:col + LANES] = y[g * tq:(g + 1) * tq].astype(o_ref.dtype)


def _flash(q, k, v, heads, hp, n_maps, groups, kd, extra=(), lam_init=0.0):
    b, s, _ = q.shape
    tq = min(TQ, s)
    rb = min(FLASH_ROW_BLOCK, tq)
    assert s % tq == 0 and tq % rb == 0 and heads % hp == 0
    nt = s // tq
    qi = jnp.asarray([i for i in range(nt) for _ in range(i + 1)], jnp.int32)
    kj = jnp.asarray([j for i in range(nt) for j in range(i + 1)], jnp.int32)
    wq = hp * n_maps * groups * kd
    wo = hp * groups * LANES
    rows = groups * tq
    in_specs = [pl.BlockSpec((1, tq, wq), lambda bi, h, p, qi_r, kj_r: (bi, qi_r[p], h)),
                pl.BlockSpec((1, tq, hp * kd), lambda bi, h, p, qi_r, kj_r: (bi, kj_r[p], h)),
                pl.BlockSpec((1, tq, hp * LANES), lambda bi, h, p, qi_r, kj_r: (bi, kj_r[p], h))]
    for e in extra:
        in_specs.append(pl.BlockSpec(e.shape, lambda bi, h, p, qi_r, kj_r: (0, 0)))
    grid_spec = pltpu.PrefetchScalarGridSpec(
        num_scalar_prefetch=2,
        grid=(b, heads // hp, int(qi.shape[0])),
        in_specs=in_specs,
        out_specs=pl.BlockSpec((1, tq, wo), lambda bi, h, p, qi_r, kj_r: (bi, qi_r[p], h)),
        scratch_shapes=[pltpu.VMEM((hp * n_maps, rows, kd), BF16), pltpu.VMEM((hp, tq, 2 * LANES), BF16),
                        pltpu.VMEM((hp * n_maps, rows, LANES), F32),
                        pltpu.VMEM((hp * n_maps, rows, 2 * LANES), F32)],
    )
    return pl.pallas_call(
        functools.partial(_flash_kernel, hp=hp, n_maps=n_maps, groups=groups, kd=kd, lam_init=lam_init, rb=rb),
        grid_spec=grid_spec,
        out_shape=jax.ShapeDtypeStruct((b, s, heads * groups * LANES), BF16),
        compiler_params=_params("parallel", "parallel", "arbitrary"),
        name="flash_da" if n_maps == 2 else "flash_mla",
    )(qi, kj, q, k, v, *extra)


def _decode_da_kernel(pt_ref, q_ref, ks_ref, vs_ref, lam_ref, subg_ref, *rest, pages, lam_init):
    k_refs, v_refs = rest[:pages], rest[pages:2 * pages]
    o_ref, qst_ref, kb_ref, vb_ref, m_ref, l_ref, acc_ref = rest[2 * pages:]
    step = pl.program_id(1)
    t_new = q_ref.shape[1]
    hrows = 4 * t_new

    def attend(scores, values, mask):
        s = jnp.concatenate([scores(h) for h in range(DA_KV_HEADS)], axis=0)
        if mask is not None:
            s = jnp.where(mask, s, MASKED)
        p, corr, m_new, l_new = _online_update(s, m_ref[...], l_ref[...])
        for h in range(DA_KV_HEADS):
            rs = slice(h * hrows, (h + 1) * hrows)
            acc_ref[rs] = acc_ref[rs] * corr[rs] + jnp.dot(p[rs].astype(BF16), values(h),
                                                          preferred_element_type=F32)
        m_ref[...] = m_new
        l_ref[...] = l_new

    @pl.when(step == 0)
    def _init():
        q = q_ref[0]
        for h in range(DA_KV_HEADS):
            parts = [q[:, ((h * 2 + mp) * 2 + g) * LANES:((h * 2 + mp) * 2 + g + 1) * LANES]
                     for mp in range(2) for g in range(2)]
            qst_ref[h] = jnp.concatenate(parts, axis=0).astype(BF16)
        m_ref[...] = jnp.full(m_ref.shape, MASKED, F32)
        l_ref[...] = jnp.zeros(l_ref.shape, F32)
        acc_ref[...] = jnp.zeros(acc_ref.shape, F32)
        pad = jnp.zeros((PAGE - t_new, DA_KV_HEADS * LANES), F32)
        k_own = jnp.concatenate([ks_ref[0], pad], axis=0).astype(BF16)
        v_own = jnp.concatenate([vs_ref[0], pad], axis=0).astype(BF16)
        r = lax.broadcasted_iota(jnp.int32, (DA_KV_HEADS * hrows, PAGE), 0)
        c = lax.broadcasted_iota(jnp.int32, (DA_KV_HEADS * hrows, PAGE), 1)
        attend(lambda h: lax.dot_general(qst_ref[h], k_own[:, h * LANES:(h + 1) * LANES], _NT,
                                         preferred_element_type=F32),
               lambda h: v_own[:, h * LANES:(h + 1) * LANES],
               c <= jnp.bitwise_and(r, t_new - 1))

    for r in range(pages):
        kb_ref[:, r * PAGE:(r + 1) * PAGE] = k_refs[r][0, 0].astype(BF16)
        for h in range(DA_KV_HEADS):
            vb_ref[h, r * PAGE:(r + 1) * PAGE, :] = (
                v_refs[r][0, 0, pl.ds(h, PAGE, stride=DA_KV_HEADS), :].astype(BF16))
    attend(lambda h: jnp.dot(qst_ref[h], kb_ref[h * LANES:(h + 1) * LANES, :], preferred_element_type=F32),
           lambda h: vb_ref[h], None)

    @pl.when(step == pl.num_programs(1) - 1)
    def _finish():
        o = acc_ref[...] / l_ref[...]
        lam = _da_lambda(lam_ref, lam_init)
        for h in range(DA_KV_HEADS):
            d = o[h * hrows:h * hrows + 2 * t_new] - lam * o[h * hrows + 2 * t_new:(h + 1) * hrows]
            y = _rms_rows(d, subg_ref[...]) * (1.0 - lam_init)
            for g in range(2):
                o_ref[0, :, (h * 2 + g) * LANES:(h * 2 + g + 1) * LANES] = y[g * t_new:(g + 1) * t_new]


def _decode_da(page_table, q, k_own, v_own, lam_p, subg, cache_kt, cache_v, layer, lam_init):
    nb, t_new, wq = q.shape
    n_pages = page_table.shape[1]
    pages = DA_PAGES_PER_STEP
    assert n_pages % pages == 0 and t_new == SUBLANES
    wkv = DA_KV_HEADS * LANES

    def page_spec(r):
        return pl.BlockSpec((1, 1, wkv, PAGE),
                            lambda b, j, pt: (layer, pt[b * n_pages + j * pages + r], 0, 0))

    per_seq = lambda w: pl.BlockSpec((1, t_new, w), lambda b, j, pt: (b, 0, 0))
    const = lambda a: pl.BlockSpec(a.shape, lambda b, j, pt: (0, 0))
    rows = DA_KV_HEADS * 4 * t_new
    grid_spec = pltpu.PrefetchScalarGridSpec(
        num_scalar_prefetch=1,
        grid=(nb, n_pages // pages),
        in_specs=([per_seq(wq), per_seq(wkv), per_seq(wkv), const(lam_p), const(subg)]
                  + [page_spec(r) for r in range(pages)] * 2),
        out_specs=per_seq(2 * wkv),
        scratch_shapes=[pltpu.VMEM((DA_KV_HEADS, 4 * t_new, LANES), BF16),
                        pltpu.VMEM((wkv, pages * PAGE), BF16),
                        pltpu.VMEM((DA_KV_HEADS, pages * PAGE, LANES), BF16),
                        pltpu.VMEM((rows, 1), F32), pltpu.VMEM((rows, 1), F32),
                        pltpu.VMEM((rows, LANES), F32)],
    )
    return pl.pallas_call(
        functools.partial(_decode_da_kernel, pages=pages, lam_init=lam_init),
        grid_spec=grid_spec,
        out_shape=jax.ShapeDtypeStruct((nb, t_new, 2 * wkv), F32),
        compiler_params=_params("parallel", "arbitrary"),
        name="decode_da",
    )(page_table.reshape(-1), q, k_own, v_own, lam_p, subg,
      *([cache_kt] * pages), *([cache_v] * pages))


def _decode_mla_kernel(pt_ref, q_ref, cs_ref, rs_ref, wukt_ref, wuv_ref, *rest, pages, chunk):
    c_refs, r_refs = rest[:pages], rest[pages:2 * pages]
    o_ref, lhs_ref, qr_ref, cc_ref, rb_ref, m_ref, l_ref, acc_ref = rest[2 * pages:]
    step = pl.program_id(1)
    t_new = q_ref.shape[1]
    nq = MLA_HEADS * t_new
    ne = MLA_HEADS * MLA_NOPE

    def scores(tok, bp):
        res = lax.dot_general(lhs_ref[...], tok, _NT, preferred_element_type=F32)
        e2 = res[:ne] * res[:ne]
        a = res[ne:]
        s_parts = []
        for h in range(MLA_HEADS):
            ms = jnp.sum(e2[h * MLA_NOPE:(h + 1) * MLA_NOPE], axis=0, keepdims=True) * (1.0 / MLA_NOPE)
            s_parts.append(a[h * t_new:(h + 1) * t_new] * lax.rsqrt(ms + EPS)
                           + bp[h * t_new:(h + 1) * t_new])
        return jnp.concatenate(s_parts, axis=0)

    def fold(s, tok):
        cols = [s[:, c * LANES:(c + 1) * LANES] for c in range(s.shape[1] // LANES)]
        mx = cols[0]
        for col in cols[1:]:
            mx = jnp.maximum(mx, col)
        m_prev = m_ref[...]
        m_new = jnp.maximum(m_prev, jnp.max(mx, axis=-1, keepdims=True))
        corr = jnp.exp(m_prev - m_new)
        p_cols = [jnp.exp(col - m_new) for col in cols]
        tot = p_cols[0]
        for pc in p_cols[1:]:
            tot = tot + pc
        l_ref[...] = l_ref[...] * corr + jnp.sum(tot, axis=-1, keepdims=True)
        p = jnp.concatenate([pc.astype(BF16) for pc in p_cols], axis=1)
        acc_ref[...] = acc_ref[...] * corr + jnp.dot(p, tok, preferred_element_type=F32)
        m_ref[...] = m_new

    @pl.when(step == 0)
    def _init():
        q = q_ref[0]
        qa = jnp.concatenate([q[:, h * 2 * LANES:h * 2 * LANES + LANES] for h in range(MLA_HEADS)], axis=0)
        qr = jnp.concatenate([q[:, h * 2 * LANES + LANES:(h + 1) * 2 * LANES] for h in range(MLA_HEADS)], axis=0)
        lhs_ref[0:ne, :] = wukt_ref[...]
        lhs_ref[ne:ne + nq, :] = qa.astype(BF16)
        qr_ref[...] = qr.astype(BF16)
        rb_ref[MLA_ROPE:, :] = jnp.zeros((LANES - MLA_ROPE, rb_ref.shape[1]), BF16)
        m_ref[...] = jnp.full(m_ref.shape, MASKED, F32)
        l_ref[...] = jnp.zeros(l_ref.shape, F32)
        acc_ref[...] = jnp.zeros(acc_ref.shape, F32)
        pad = jnp.zeros((PAGE - t_new, LANES), F32)
        c_own = jnp.concatenate([cs_ref[0], pad], axis=0).astype(BF16)
        r_own = jnp.concatenate([rs_ref[0], pad], axis=0).astype(BF16)
        r = lax.broadcasted_iota(jnp.int32, (nq, PAGE), 0)
        c = lax.broadcasted_iota(jnp.int32, (nq, PAGE), 1)
        s_own = scores(c_own, lax.dot_general(qr_ref[...], r_own, _NT, preferred_element_type=F32))
        fold(jnp.where(c <= jnp.bitwise_and(r, t_new - 1), s_own, MASKED), c_own)

    for r in range(pages):
        cc_ref[r * PAGE:(r + 1) * PAGE, :] = c_refs[r][0, 0].astype(BF16)
        rb_ref[0:MLA_ROPE, r * PAGE:(r + 1) * PAGE] = r_refs[r][0, 0].astype(BF16)
    s_chunks = []
    for ck in range(pages * PAGE // chunk):
        cs = slice(ck * chunk, (ck + 1) * chunk)
        s_chunks.append(scores(cc_ref[cs, :], jnp.dot(qr_ref[...], rb_ref[:, cs], preferred_element_type=F32)))
    fold(jnp.concatenate(s_chunks, axis=1), cc_ref[...])

    @pl.when(step == pl.num_programs(1) - 1)
    def _finish():
        o_lat = (acc_ref[...] / l_ref[...]).astype(BF16)
        o_all = jnp.dot(o_lat, wuv_ref[...], preferred_element_type=F32)
        for h in range(MLA_HEADS):
            o_ref[0, :, h * MLA_V:(h + 1) * MLA_V] = o_all[h * t_new:(h + 1) * t_new,
                                                           h * MLA_V:(h + 1) * MLA_V]


def _decode_mla(page_table, q, c_own, r_own, wukt, wuv, cache_c, cache_rt, layer):
    nb, t_new, wq = q.shape
    n_pages = page_table.shape[1]
    pages = MLA_PAGES_PER_STEP
    chunk = MLA_CHUNK
    assert n_pages % pages == 0 and (pages * PAGE) % chunk == 0 and t_new == SUBLANES

    def page_spec(r, rows):
        return pl.BlockSpec((1, 1, rows, PAGE),
                            lambda b, j, pt: (layer, pt[b * n_pages + j * pages + r], 0, 0))

    per_seq = lambda w: pl.BlockSpec((1, t_new, w), lambda b, j, pt: (b, 0, 0))
    const = lambda a: pl.BlockSpec(a.shape, lambda b, j, pt: (0, 0))
    nq = MLA_HEADS * t_new
    grid_spec = pltpu.PrefetchScalarGridSpec(
        num_scalar_prefetch=1,
        grid=(nb, n_pages // pages),
        in_specs=([per_seq(wq), per_seq(MLA_KV_RANK), per_seq(LANES), const(wukt), const(wuv)]
                  + [page_spec(r, PAGE) for r in range(pages)]
                  + [page_spec(r, MLA_ROPE) for r in range(pages)]),
        out_specs=per_seq(MLA_HEADS * MLA_V),
        scratch_shapes=[pltpu.VMEM((MLA_HEADS * MLA_NOPE + nq, MLA_KV_RANK), BF16),
                        pltpu.VMEM((nq, LANES), BF16),
                        pltpu.VMEM((pages * PAGE, MLA_KV_RANK), BF16),
                        pltpu.VMEM((LANES, pages * PAGE), BF16),
                        pltpu.VMEM((nq, LANES), F32), pltpu.VMEM((nq, LANES), F32),
                        pltpu.VMEM((nq, MLA_KV_RANK), F32)],
    )
    return pl.pallas_call(
        functools.partial(_decode_mla_kernel, pages=pages, chunk=chunk),
        grid_spec=grid_spec,
        out_shape=jax.ShapeDtypeStruct((nb, t_new, MLA_HEADS * MLA_V), F32),
        compiler_params=_params("parallel", "arbitrary"),
        name="decode_mla",
    )(page_table.reshape(-1), q, c_own, r_own, wukt, wuv,
      *([cache_c] * pages), *([cache_rt] * pages))


def _ffn_kernel(x_ref, a_ref, wo_ref, gn_ref, wgu_ref, cw_ref, cb_ref, wd_ref, prev_ref,
                xo_ref, go_ref, *carry, tiles_per_seq):
    i = pl.program_id(0)
    tm = x_ref.shape[0]
    nf, tf = wd_ref.shape[0], wd_ref.shape[1]

    x1 = x_ref[...] + jnp.dot(a_ref[...].astype(BF16), wo_ref[...], preferred_element_type=F32)
    h = _rms_rows(x1, gn_ref[...]).astype(BF16)
    row = lax.broadcasted_iota(jnp.int32, (tm, tf), 0)
    t = row if tiles_per_seq else jnp.bitwise_and(row, SUBLANES - 1)
    cw, cb = cw_ref[...], cb_ref[...]
    if tiles_per_seq:
        (carry_ref,) = carry

        @pl.when(i % tiles_per_seq == 0)
        def _seq_start():
            for c in range(nf):
                carry_ref[c, SUBLANES - 2:SUBLANES, :] = prev_ref[0, :, c * tf:(c + 1) * tf]

    out = x1
    for c in range(nf):
        cs = slice(c * tf, (c + 1) * tf)
        g = jnp.dot(h, wgu_ref[:, cs], preferred_element_type=F32)
        u = jnp.dot(h, wgu_ref[:, D_FF + c * tf:D_FF + (c + 1) * tf], preferred_element_type=F32)
        if tiles_per_seq:
            tail = carry_ref[c]
            p0, p1 = tail[SUBLANES - 2:SUBLANES - 1], tail[SUBLANES - 1:SUBLANES]
            carry_ref[c] = g[tm - SUBLANES:]
            go_ref[0, :, cs] = g[tm - SUBLANES:]
        else:
            nseq = tm // SUBLANES
            prev = prev_ref[:, :, cs]
            spread = lambda v: jnp.broadcast_to(v, (nseq, SUBLANES, tf)).reshape(tm, tf)
            p0, p1 = spread(prev[:, 0:1, :]), spread(prev[:, 1:2, :])
            go_ref[:, cs] = g
        g1 = jnp.where(t == 0, p1, pltpu.roll(g, 1, axis=0))
        g2 = jnp.where(t == 0, p0, jnp.where(t == 1, p1, pltpu.roll(g, 2, axis=0)))
        gc = cb[:, cs] + cw[0:1, cs] * g2 + cw[1:2, cs] * g1 + cw[2:3, cs] * g
        act = gc * (1.0 / (1.0 + jnp.exp(-gc))) * u
        out = out + jnp.dot(act.astype(BF16), wd_ref[c], preferred_element_type=F32)
    xo_ref[...] = out


def _ffn(x, attn, w_out, gn, w_gu, conv_w, conv_b, w_down, prev, layer, tm, tiles_per_seq):
    n = x.shape[0]
    nf, tf = w_down.shape[1], w_down.shape[2]
    nt = n // tm
    row = lambda i: (i, 0)
    const2 = lambda i: (0, 0)
    of_layer3 = lambda i: (layer, 0, 0)
    of_layer4 = lambda i: (layer, 0, 0, 0)
    once = dict(pipeline_mode=pl.Buffered(1))
    if tiles_per_seq:
        prev_spec = pl.BlockSpec((1, 2, D_FF), lambda i: (i // tiles_per_seq, 0, 0))
        go_spec = pl.BlockSpec((1, SUBLANES, D_FF), lambda i: (i, 0, 0))
        go_shape = jax.ShapeDtypeStruct((nt, SUBLANES, D_FF), F32)
        scratch = [pltpu.VMEM((nf, SUBLANES, tf), F32)]
    else:
        prev_spec = pl.BlockSpec((tm // SUBLANES, 2, D_FF), lambda i: (i, 0, 0))
        go_spec = pl.BlockSpec((tm, D_FF), row)
        go_shape = jax.ShapeDtypeStruct((n, D_FF), F32)
        scratch = []
    return pl.pallas_call(
        functools.partial(_ffn_kernel, tiles_per_seq=tiles_per_seq),
        grid=(nt,),
        in_specs=[pl.BlockSpec((tm, D_MODEL), row), pl.BlockSpec((tm, D_MODEL), row),
                  pl.BlockSpec((D_MODEL, D_MODEL), const2, **once), pl.BlockSpec((1, D_MODEL), const2),
                  pl.BlockSpec((None,) + w_gu.shape[1:], of_layer3, **once),
                  pl.BlockSpec((3, D_FF), const2), pl.BlockSpec((1, D_FF), const2),
                  pl.BlockSpec((None,) + w_down.shape[1:], of_layer4, **once), prev_spec],
        out_specs=[pl.BlockSpec((tm, D_MODEL), row), go_spec],
        out_shape=[jax.ShapeDtypeStruct((n, D_MODEL), F32), go_shape],
        scratch_shapes=scratch,
        compiler_params=_params("arbitrary"),
        name="conv_ffn_seq" if tiles_per_seq else "conv_ffn_blocks",
    )(x, attn, w_out, gn, w_gu, conv_w, conv_b, w_down, prev)


def _rope_tables(pos, rot, theta, rows):
    half = rot // 2
    inv = 1.0 / (theta ** (jnp.arange(half, dtype=F32) / half))
    ang = pos.astype(F32)[:, None] * inv[None, :]
    d = jnp.arange(LANES) % 64
    idx = d % half
    cos = jnp.where(d < rot, jnp.cos(ang)[:, idx], 1.0)
    sin = jnp.sin(ang)[:, idx]
    sin_up = jnp.where(d < half, -sin, 0.0)
    sin_dn = jnp.where((d >= half) & (d < rot), sin, 0.0)
    reps = rows // pos.shape[0] if pos.shape[0] < rows else 1
    return tuple(jnp.tile(t.astype(F32), (reps, 1)) for t in (cos, sin_up, sin_dn))


def _row(v):
    return v.reshape(1, -1).astype(F32)


def _tile2(v):
    return jnp.tile(v.astype(F32), 2).reshape(1, LANES)


def kernel(x_prompt, x_sample, cache_da_k, cache_da_v, cache_mla_ckv, cache_mla_kpe, state_conv, page_table,
           attn_norm, ffn_norm, da_w_in, da_qk_norm, da_lambda, da_sub_norm, da_w_out,
           mla_w_in, mla_q_a_norm, mla_kv_a_norm, mla_w_uq, mla_w_ukv, mla_qk_norm, mla_w_out,
           ffn_w_gu, ffn_conv_w, ffn_conv_b, ffn_w_down):
    b, s, d = x_prompt.shape
    nb, t_new, _ = x_sample.shape
    n_pages = page_table.shape[1]
    depth = ffn_w_gu.shape[0]
    n_pool = cache_da_k.shape[1]
    assert d == D_MODEL and s % TM_FFN_PROMPT == 0 and (nb * t_new) % TM_PROJ == 0

    pos_p = jnp.arange(s)
    pos_s = n_pages * PAGE + jnp.arange(t_new)
    da_tab_p = _rope_tables(pos_p, 2 * DA_ROT_HALF, DA_THETA, TM_PROJ)
    da_tab_s = _rope_tables(pos_s, 2 * DA_ROT_HALF, DA_THETA, TM_PROJ)
    mla_tab_p = _rope_tables(pos_p, MLA_ROPE, MLA_THETA, TM_PROJ)
    mla_tab_s = _rope_tables(pos_s, MLA_ROPE, MLA_THETA, TM_PROJ)

    ck = jnp.transpose(cache_da_k, (0, 1, 3, 4, 5, 2)).reshape(-1, n_pool, DA_KV_HEADS * LANES, PAGE)
    cv = cache_da_v.reshape(-1, n_pool, PAGE * DA_KV_HEADS, LANES)
    cr = jnp.transpose(cache_mla_kpe, (0, 1, 3, 2))
    zero_prev = jnp.zeros((b, 2, D_FF), F32)
    nf = D_FF // TF
    w_gu = ffn_w_gu.astype(BF16)
    w_down = ffn_w_down.astype(BF16).reshape(depth, nf, TF, D_MODEL)

    xp = x_prompt.reshape(b * s, d)
    xs = x_sample.reshape(nb * t_new, d)
    out = {name: [] for name in ("dkp", "dvp", "ckp", "kep", "cvp", "dks", "dvs", "cks", "kes", "cvs")}
    for i in range(depth):
        li = i // N_MIXERS
        gn = _row(attn_norm[i])
        if i % N_MIXERS == 0:
            lam_init = 0.8 - 0.6 * math.exp(-0.3 * i)
            w_in = da_w_in[li].astype(BF16)
            gq, gk = _tile2(da_qk_norm[li, 0]), _tile2(da_qk_norm[li, 1])
            lam_p, subg = da_lambda[li].astype(F32), _row(da_sub_norm[li])
            w_out = da_w_out[li].astype(BF16)
            qp, kp, vp, kpb, vpb = _proj_da(xp, gn, w_in, gq, gk, da_tab_p, BF16)
            ap = _flash(qp.reshape(b, s, -1), kpb.reshape(b, s, -1), vpb.reshape(b, s, -1),
                        DA_KV_HEADS, 1, 2, 2, LANES, extra=(lam_p, subg), lam_init=lam_init)
            qs, ks, vs, _, _ = _proj_da(xs, gn, w_in, gq, gk, da_tab_s, F32)
            a_s = _decode_da(page_table, qs.reshape(nb, t_new, -1), ks.reshape(nb, t_new, -1),
                             vs.reshape(nb, t_new, -1), lam_p, subg, ck, cv, li, lam_init)
            out["dkp"].append(kp.reshape(b, s, DA_KV_HEADS, 2, DA_HEAD_DIM))
            out["dvp"].append(vp.reshape(b, s, DA_KV_HEADS, 2 * DA_HEAD_DIM))
            out["dks"].append(ks.reshape(nb, t_new, DA_KV_HEADS, 2, DA_HEAD_DIM))
            out["dvs"].append(vs.reshape(nb, t_new, DA_KV_HEADS, 2 * DA_HEAD_DIM))
        else:
            w_in = jnp.pad(mla_w_in[li], ((0, 0), (0, 4 * LANES - mla_w_in.shape[2]))).astype(BF16)
            w_uq = mla_w_uq[li].reshape(MLA_Q_RANK, MLA_HEADS, MLA_NOPE + MLA_ROPE)
            w_uq = jnp.concatenate([w_uq[:, :, :MLA_NOPE].reshape(MLA_Q_RANK, -1),
                                    w_uq[:, :, MLA_NOPE:].reshape(MLA_Q_RANK, -1)], axis=1).astype(BF16)
            w_uk = mla_w_ukv[li, :, :, :MLA_NOPE]
            w_uv = mla_w_ukv[li, :, :, MLA_NOPE:].reshape(MLA_KV_RANK, -1)
            w_ukv = jnp.concatenate([w_uk.reshape(MLA_KV_RANK, -1), w_uv], axis=1).astype(BF16)
            w_ukt = jnp.transpose(w_uk, (1, 2, 0)).astype(BF16)
            qag, kvag = _row(mla_q_a_norm[li]), _row(mla_kv_a_norm[li])
            gqn, kg = _row(mla_qk_norm[li, 0, :MLA_NOPE]), _row(mla_qk_norm[li, 1, :MLA_NOPE])
            gqr, gkr = _tile2(mla_qk_norm[li, 0, MLA_NOPE:]), _tile2(mla_qk_norm[li, 1, MLA_NOPE:])
            w_out = mla_w_out[li].astype(BF16)
            qp, ckvp, krp, kcat, vpb = _proj_mla(xp, gn, w_in, qag, kvag, w_uq, gqn, gqr, gkr, kg,
                                                 mla_tab_p, w_ukv, absorb=False)
            ap = _flash(qp.reshape(b, s, -1), kcat.reshape(b, s, -1), vpb.reshape(b, s, -1),
                        MLA_HEADS, FLASH_MLA_HEADS_PER_STEP, 1, 1, 2 * LANES)
            qs, ckvs, krs, krs_pad = _proj_mla(xs, gn, w_in, qag, kvag, w_uq, gqn, gqr, gkr, kg,
                                               mla_tab_s, w_ukt, absorb=True)
            a_s = _decode_mla(page_table, qs.reshape(nb, t_new, -1), ckvs.reshape(nb, t_new, -1),
                              krs_pad.reshape(nb, t_new, -1), w_ukt.reshape(-1, MLA_KV_RANK),
                              w_uv.astype(BF16), cache_mla_ckv, cr, li)
            out["ckp"].append(ckvp.reshape(b, s, MLA_KV_RANK))
            out["kep"].append(krp.reshape(b, s, MLA_ROPE))
            out["cks"].append(ckvs.reshape(nb, t_new, MLA_KV_RANK))
            out["kes"].append(krs.reshape(nb, t_new, MLA_ROPE))
        fgn = _row(ffn_norm[i])
        cw, cb = ffn_conv_w[i].astype(F32), _row(ffn_conv_b[i])
        tps = s // TM_FFN_PROMPT
        xp, gp = _ffn(xp, ap.reshape(b * s, d), w_out, fgn, w_gu, cw, cb, w_down, zero_prev, i,
                      TM_FFN_PROMPT, tps)
        xs, gs = _ffn(xs, a_s.reshape(nb * t_new, d), w_out, fgn, w_gu, cw, cb, w_down, state_conv[i], i,
                      TM_FFN_SAMPLE, 0)
        out["cvp"].append(gp.reshape(b, tps, SUBLANES, D_FF)[:, -1, SUBLANES - 2:, :])
        out["cvs"].append(gs.reshape(nb, t_new, D_FF)[:, t_new - 2:, :])
    st = jnp.stack
    return (xp.reshape(b, s, d), xs.reshape(nb, t_new, d), st(out["dkp"]), st(out["dvp"]), st(out["ckp"]),
            st(out["kep"]), st(out["cvp"]), st(out["dks"]), st(out["dvs"]), st(out["cks"]), st(out["kes"]),
            st(out["cvs"]))
```
